```python
import jax, jax.numpy as jnp
from jax import lax
import numpy as np

D_MODEL = 1024
BATCH = 4
SEQ = 4096
DEPTH = 2
DEC_BATCH = 32
DEC_SEQ = 16
PAST_LEN = 1024

CHUNK = 64
N_META = 16
D_POOL = 256
POOL_WINDOWS = (2, 4, 8, 16)
N_POOL_GROUPS = 4
POOL_GROUP = D_POOL // N_POOL_GROUPS
POOL_STATE = 15
D_CONV = 256
CONV_W = 3
N_HEADS = 8
HEAD_DIM = 64
D_ATTN = N_HEADS * HEAD_DIM
Q_BLOCK = 128
N_BRANCH = 3
D_FF = 2816
RMS_EPS = 1e-6
IN_COLS = D_POOL + 3 * D_CONV + 3 * D_ATTN + N_BRANCH * D_MODEL

kernel_name = "hybrid_pool_conv_stickbreak_streaming_step"


def rmsnorm(x, g):
    xf = x.astype(jnp.float32)
    y = xf * lax.rsqrt(jnp.mean(xf * xf, axis=-1, keepdims=True) + RMS_EPS)
    return (y * g.astype(jnp.float32)).astype(x.dtype)


def swiglu(x, w_gate, w_up, w_down):
    return (jax.nn.silu(x @ w_gate) * (x @ w_up)) @ w_down


def pool_mix(a, prev, pos0, w_group, scale):
    bsz, t, _ = a.shape
    ext = jnp.concatenate([prev, a], axis=1)
    c = jnp.cumsum(ext.astype(jnp.float32), axis=1)
    c = jnp.pad(c, ((0, 0), (1, 0), (0, 0)))
    cur = c[:, POOL_STATE + 1:]
    pos = pos0 + jnp.arange(t)
    means = []
    for g, w in enumerate(POOL_WINDOWS):
        sl = slice(g * POOL_GROUP, (g + 1) * POOL_GROUP)
        s = cur[..., sl] - c[:, POOL_STATE + 1 - w:POOL_STATE + 1 - w + t, sl]
        cnt = jnp.minimum(pos + 1, w).astype(jnp.float32)[None, :, None]
        means.append(s / cnt)
    p = (jnp.concatenate(means, axis=-1) - a.astype(jnp.float32)).astype(a.dtype)
    p = p.reshape(bsz, t, N_POOL_GROUPS, POOL_GROUP)
    p = jnp.einsum('btgc,gcd->btgd', p, w_group).reshape(bsz, t, D_POOL) * scale
    return p, ext[:, -POOL_STATE:]


def conv_mix(xb, gate_b, gate_c, prev, conv_w):
    t = xb.shape[1]
    ext = jnp.concatenate([prev, gate_c * xb], axis=1)
    y = ext[:, 0:t] * conv_w[0]
    for j in range(1, CONV_W):
        y = y + ext[:, j:j + t] * conv_w[j]
    return gate_b * y, ext[:, -(CONV_W - 1):]


def stick_breaking(q, k, v, q_pos0):
    tq, tk = q.shape[1], k.shape[1]
    outs = []
    for qs in range(0, tq, Q_BLOCK):
        qe = min(qs + Q_BLOCK, tq)
        n_keys = max(1, min(tk, q_pos0 + qe - 1))
        z = jnp.einsum('bqhd,bkhd->bhqk', q[:, qs:qe].astype(jnp.float32),
                       k[:, :n_keys].astype(jnp.float32)) * (HEAD_DIM ** -0.5)
        tpos = q_pos0 + jnp.arange(qs, qe)
        spos = jnp.arange(n_keys)
        causal = spos[None, :] < tpos[:, None]
        log_keep = jnp.where(causal, jax.nn.log_sigmoid(-z), 0.0)
        later = lax.cumsum(log_keep, axis=3, reverse=True) - log_keep
        a = jnp.where(causal, jnp.exp(jax.nn.log_sigmoid(z) + later), 0.0)
        o = jnp.einsum('bhqk,bkhd->bqhd', a, v[:, :n_keys].astype(jnp.float32))
        outs.append(o.astype(q.dtype))
    return jnp.concatenate(outs, axis=1)


def trunk_layer(x, pool_prev, conv_prev, k_past, v_past, w):
    (n1, f1g, f1u, f1d, nm, w_in, pool_w, pool_s, pool_proj, conv_w, conv_proj,
     qn, kn, attn_proj, w_out, n2, f2g, f2u, f2d) = w
    bsz, t, _ = x.shape
    pos0 = k_past.shape[1]
    h = x + 0.5 * swiglu(rmsnorm(x, n1), f1g, f1u, f1d)
    u = rmsnorm(h, nm)
    proj = u @ w_in
    cuts = np.cumsum([D_POOL, D_CONV, D_CONV, D_CONV, D_ATTN, D_ATTN, D_ATTN]).tolist()
    a_in, xb, gb, gc, q, k, v, g_logits = jnp.split(proj, cuts, axis=-1)
    y_a, pool_state = pool_mix(a_in, pool_prev, pos0, pool_w, pool_s)
    y_a = y_a @ pool_proj
    y_b, conv_state = conv_mix(xb, gb, gc, conv_prev, conv_w)
    y_b = y_b @ conv_proj
    q = rmsnorm(q.reshape(bsz, t, N_HEADS, HEAD_DIM), qn)
    k = rmsnorm(k.reshape(bsz, t, N_HEADS, HEAD_DIM), kn)
    v = v.reshape(bsz, t, N_HEADS, HEAD_DIM)
    k_all = jnp.concatenate([k_past, k], axis=1)
    v_all = jnp.concatenate([v_past, v], axis=1)
    o = stick_breaking(q, k_all, v_all, pos0)
    y_c = o.reshape(bsz, t, D_ATTN) @ attn_proj
    g = jax.nn.sigmoid(g_logits.astype(jnp.float32)).astype(x.dtype).reshape(bsz, t, N_BRANCH, D_MODEL)
    mixed = g[:, :, 0] * y_a + g[:, :, 1] * y_b + g[:, :, 2] * y_c
    h = h + mixed @ w_out
    h = h + 0.5 * swiglu(rmsnorm(h, n2), f2g, f2u, f2d)
    return h, k, v, pool_state, conv_state


def setup_inputs(seed: int = 0) -> dict:
    key = jax.random.key(seed)
    ks = jax.random.split(key, 32)
    f32 = jnp.float32

    def nrm(k, shape, scale):
        return jax.random.normal(k, shape, f32) * scale

    def gain(k, shape):
        return 1.0 + 0.05 * jax.random.normal(k, shape, f32)

    return {
        "x_prompt": nrm(ks[0], (BATCH, SEQ, D_MODEL), 1.0),
        "x_sample": nrm(ks[1], (DEC_BATCH, DEC_SEQ, D_MODEL), 1.0),
        "cache_k": nrm(ks[2], (DEPTH, DEC_BATCH, PAST_LEN, N_HEADS, HEAD_DIM), 1.0),
        "cache_v": nrm(ks[3], (DEPTH, DEC_BATCH, PAST_LEN, N_HEADS, HEAD_DIM), 1.0),
        "state_pool": nrm(ks[4], (DEPTH, DEC_BATCH, POOL_STATE, D_POOL), 1.0),
        "state_conv": nrm(ks[5], (DEPTH, DEC_BATCH, CONV_W - 1, D_CONV), 1.0),
        "meta": nrm(ks[6], (N_META, D_MODEL), 1.0),
        "ffn1_norm": gain(ks[7], (DEPTH, D_MODEL)),
        "ffn1_w_gate": nrm(ks[8], (DEPTH, D_MODEL, D_FF), D_MODEL ** -0.5),
        "ffn1_w_up": nrm(ks[9], (DEPTH, D_MODEL, D_FF), D_MODEL ** -0.5),
        "ffn1_w_down": nrm(ks[10], (DEPTH, D_FF, D_MODEL), D_FF ** -0.5),
        "mix_norm": gain(ks[11], (DEPTH, D_MODEL)),
        "w_in": nrm(ks[12], (DEPTH, D_MODEL, IN_COLS), D_MODEL ** -0.5),
        "pool_w": nrm(ks[13], (DEPTH, N_POOL_GROUPS, POOL_GROUP, POOL_GROUP), POOL_GROUP ** -0.5),
        "pool_scale": gain(ks[14], (DEPTH, D_POOL)),
        "pool_proj": nrm(ks[15], (DEPTH, D_POOL, D_MODEL), D_POOL ** -0.5),
        "conv_w": nrm(ks[16], (DEPTH, CONV_W, D_CONV), CONV_W ** -0.5),
        "conv_proj": nrm(ks[17], (DEPTH, D_CONV, D_MODEL), D_CONV ** -0.5),
        "q_norm": gain(ks[18], (DEPTH, N_HEADS, HEAD_DIM)),
        "k_norm": gain(ks[19], (DEPTH, N_HEADS, HEAD_DIM)),
        "attn_proj": nrm(ks[20], (DEPTH, D_ATTN, D_MODEL), D_ATTN ** -0.5),
        "w_out": nrm(ks[21], (DEPTH, D_MODEL, D_MODEL), D_MODEL ** -0.5),
        "ffn2_norm": gain(ks[22], (DEPTH, D_MODEL)),
        "ffn2_w_gate": nrm(ks[23], (DEPTH, D_MODEL, D_FF), D_MODEL ** -0.5),
        "ffn2_w_up": nrm(ks[24], (DEPTH, D_MODEL, D_FF), D_MODEL ** -0.5),
        "ffn2_w_down": nrm(ks[25], (DEPTH, D_FF, D_MODEL), D_FF ** -0.5),
    }


def reference(x_prompt, x_sample, cache_k, cache_v, state_pool, state_conv, meta,
              ffn1_norm, ffn1_w_gate, ffn1_w_up, ffn1_w_down, mix_norm, w_in,
              pool_w, pool_scale, pool_proj, conv_w, conv_proj, q_norm, k_norm,
              attn_proj, w_out, ffn2_norm, ffn2_w_gate, ffn2_w_up, ffn2_w_down):
    bsz = x_prompt.shape[0]
    dt = x_prompt.dtype
    meta_b = jnp.broadcast_to(meta[None].astype(dt), (bsz, N_META, D_MODEL))
    hp = jnp.concatenate([meta_b, x_prompt], axis=1)
    hs = x_sample
    zero_pool = jnp.zeros((bsz, POOL_STATE, D_POOL), dt)
    zero_conv = jnp.zeros((bsz, CONV_W - 1, D_CONV), dt)
    zero_kv = jnp.zeros((bsz, 0, N_HEADS, HEAD_DIM), dt)
    kp, vp, pp, cp, ks_, vs_, ps_, cs_ = [], [], [], [], [], [], [], []
    for l in range(DEPTH):
        w = (ffn1_norm[l], ffn1_w_gate[l], ffn1_w_up[l], ffn1_w_down[l], mix_norm[l], w_in[l],
             pool_w[l], pool_scale[l], pool_proj[l], conv_w[l], conv_proj[l],
             q_norm[l], k_norm[l], attn_proj[l], w_out[l],
             ffn2_norm[l], ffn2_w_gate[l], ffn2_w_up[l], ffn2_w_down[l])
        hp, k_new, v_new, pool_new, conv_new = trunk_layer(hp, zero_pool, zero_conv, zero_kv, zero_kv, w)
        kp.append(k_new); vp.append(v_new); pp.append(pool_new); cp.append(conv_new)
        hs, k_new, v_new, pool_new, conv_new = trunk_layer(
            hs, state_pool[l], state_conv[l], cache_k[l], cache_v[l], w)
        ks_.append(k_new); vs_.append(v_new); ps_.append(pool_new); cs_.append(conv_new)
    y_prompt = hp[:, N_META:]
    return (y_prompt, hs, jnp.stack(kp), jnp.stack(vp), jnp.stack(pp), jnp.stack(cp),
            jnp.stack(ks_), jnp.stack(vs_), jnp.stack(ps_), jnp.stack(cs_))
```

```python
import functools

import jax
import jax.numpy as jnp
import numpy as np
from jax import lax
from jax.experimental import pallas as pl
from jax.experimental.pallas import tpu as pltpu

F32 = jnp.float32
BF16 = jnp.bfloat16

D_MODEL = 1024
N_META = 16
D_POOL = 256
POOL_WINDOWS = (2, 4, 8, 16)
POOL_GROUP = 64
POOL_STATE = 15
D_CONV = 256
CONV_W = 3
N_HEADS = 8
HEAD_DIM = 64
D_ATTN = N_HEADS * HEAD_DIM
D_FF = 2816
RMS_EPS = 1e-6
N_GATE = 3 * D_MODEL
IN_COLS = N_GATE + D_POOL + 3 * D_CONV + 3 * D_ATTN

COL_GATE = 0
COL_SEQ = N_GATE
COL_Q = COL_SEQ + 1024
COL_K = COL_Q + D_ATTN
COL_V = COL_K + D_ATTN

ATT_BLOCK = 256
HALO = 16
TM = 512
TN_PROJ = 512
TF = 1408
VMEM_LIMIT = 56 * 1024 * 1024


def _params(sem):
    return pltpu.CompilerParams(dimension_semantics=sem, vmem_limit_bytes=VMEM_LIMIT)


def _rms_rows(x, gain):
    ms = jnp.mean(x * x, axis=-1, keepdims=True)
    return x * lax.rsqrt(ms + RMS_EPS) * gain


def _ffn_kernel(x_ref, g_ref, wg_ref, wu_ref, wd_ref, o_ref, xn_ref, acc_ref):
    j = pl.program_id(1)

    @pl.when(j == 0)
    def _():
        xn_ref[...] = _rms_rows(x_ref[...], g_ref[...]).astype(BF16)
        acc_ref[...] = jnp.zeros_like(acc_ref)

    xn = xn_ref[...]
    g = jnp.dot(xn, wg_ref[...], preferred_element_type=F32)
    u = jnp.dot(xn, wu_ref[...], preferred_element_type=F32)
    a = (g * jax.nn.sigmoid(g) * u).astype(BF16)
    acc_ref[...] += jnp.dot(a, wd_ref[...], preferred_element_type=F32)

    @pl.when(j == pl.num_programs(1) - 1)
    def _():
        o_ref[...] = x_ref[...] + 0.5 * acc_ref[...]


def _ffn(x, gain, wg, wu, wd):
    m = x.shape[0]
    return pl.pallas_call(
        _ffn_kernel,
        grid=(m // TM, D_FF // TF),
        in_specs=[
            pl.BlockSpec((TM, D_MODEL), lambda i, j: (i, 0)),
            pl.BlockSpec((1, D_MODEL), lambda i, j: (0, 0)),
            pl.BlockSpec((D_MODEL, TF), lambda i, j: (0, j)),
            pl.BlockSpec((D_MODEL, TF), lambda i, j: (0, j)),
            pl.BlockSpec((TF, D_MODEL), lambda i, j: (j, 0)),
        ],
        out_specs=pl.BlockSpec((TM, D_MODEL), lambda i, j: (i, 0)),
        out_shape=jax.ShapeDtypeStruct((m, D_MODEL), F32),
        scratch_shapes=[pltpu.VMEM((TM, D_MODEL), BF16), pltpu.VMEM((TM, D_MODEL), F32)],
        compiler_params=_params(("parallel", "arbitrary")),
        name="ffn",
    )(x, gain, wg, wu, wd)


def _head_rms(p, seg, gain):
    pp = p * p
    hi = pp.astype(BF16)
    lo = (pp - hi.astype(F32)).astype(BF16)
    ms = jnp.dot(hi, seg, preferred_element_type=F32) + jnp.dot(lo, seg, preferred_element_type=F32)
    return p * lax.rsqrt(ms + RMS_EPS) * gain


def _proj_kernel(h_ref, g_ref, w_ref, qn_ref, kn_ref, seg_ref, o_ref, u_ref):
    j = pl.program_id(1)

    @pl.when(j == 0)
    def _():
        u_ref[...] = _rms_rows(h_ref[...], g_ref[...]).astype(BF16)

    p = jnp.dot(u_ref[...], w_ref[...], preferred_element_type=F32)
    jq, jk = COL_Q // TN_PROJ, COL_K // TN_PROJ

    @pl.when(j < COL_SEQ // TN_PROJ)
    def _():
        o_ref[...] = jax.nn.sigmoid(p)

    @pl.when(j == jq)
    def _():
        o_ref[...] = _head_rms(p, seg_ref[...], qn_ref[...])

    @pl.when(j == jk)
    def _():
        o_ref[...] = _head_rms(p, seg_ref[...], kn_ref[...])

    @pl.when((j >= COL_SEQ // TN_PROJ) & (j != jq) & (j != jk))
    def _():
        o_ref[...] = p


def _proj(h, gain, w, qn, kn, seg):
    m = h.shape[0]
    const = lambda i, j: (0, 0)
    return pl.pallas_call(
        _proj_kernel,
        grid=(m // TM, IN_COLS // TN_PROJ),
        in_specs=[
            pl.BlockSpec((TM, D_MODEL), lambda i, j: (i, 0)),
            pl.BlockSpec((1, D_MODEL), const),
            pl.BlockSpec((D_MODEL, TN_PROJ), lambda i, j: (0, j)),
            pl.BlockSpec((1, D_ATTN), const),
            pl.BlockSpec((1, D_ATTN), const),
            pl.BlockSpec((D_ATTN, D_ATTN), const),
        ],
        out_specs=pl.BlockSpec((TM, TN_PROJ), lambda i, j: (i, j)),
        out_shape=jax.ShapeDtypeStruct((m, IN_COLS), F32),
        scratch_shapes=[pltpu.VMEM((TM, D_MODEL), BF16)],
        compiler_params=_params(("parallel", "arbitrary")),
        name="proj",
    )(h, gain, w, qn, kn, seg)


def _stick_block(qh, kblk, vblk, tri, carry, valid):
    z = lax.dot_general(qh, kblk, (((1,), (1,)), ((), ())), preferred_element_type=F32)
    sp = jnp.maximum(z, 0.0) + jnp.log(1.0 + jnp.exp(-jnp.abs(z)))
    if valid is not None:
        sp = jnp.where(valid, sp, 0.0)
    hi = sp.astype(BF16)
    lo = (sp - hi.astype(F32)).astype(BF16)
    cum = jnp.dot(hi, tri, preferred_element_type=F32) + jnp.dot(lo, tri, preferred_element_type=F32)
    a = jnp.exp(z - cum - carry)
    if valid is not None:
        a = jnp.where(valid, a, 0.0)
    out = jnp.dot(a.astype(BF16), vblk, preferred_element_type=F32)
    return out, carry + cum[:, 0:1]


def _attn_prompt_kernel(q_ref, k_ref, v_ref, tri_ref, o_ref, acc_ref, car_ref):
    i = pl.program_id(2)
    blk = ATT_BLOCK
    lane = lax.broadcasted_iota(jnp.int32, (blk, 2 * HEAD_DIM), 1)
    q2 = q_ref[...] * (HEAD_DIM ** -0.5)
    qh = [jnp.where(lane < HEAD_DIM, q2, 0.0).astype(BF16), jnp.where(lane >= HEAD_DIM, q2, 0.0).astype(BF16)]
    tri = tri_ref[...]

    def step(j, valid):
        start = pl.multiple_of(j * blk, blk)
        kblk = k_ref[pl.ds(start, blk), :].astype(BF16)
        vblk = v_ref[pl.ds(start, blk), :].astype(BF16)
        for h in range(2):
            out, car = _stick_block(qh[h], kblk, vblk, tri, car_ref[h], valid)
            acc_ref[h] += out
            car_ref[h] = car

    acc_ref[...] = jnp.zeros_like(acc_ref)
    car_ref[...] = jnp.zeros_like(car_ref)
    row = lax.broadcasted_iota(jnp.int32, (blk, blk), 0)
    col = lax.broadcasted_iota(jnp.int32, (blk, blk), 1)
    step(i, col < row)

    def body(t, c):
        step(i - 1 - t, None)
        return c

    lax.fori_loop(0, i, body, 0)
    o_ref[...] = jnp.where(lane < HEAD_DIM, acc_ref[0], acc_ref[1])


def _attn_prompt(p, tri, n_batch, t_pad):
    m = p.shape[0]
    nq = t_pad // ATT_BLOCK
    pair = 2 * HEAD_DIM
    return pl.pallas_call(
        _attn_prompt_kernel,
        grid=(n_batch, N_HEADS // 2, nq),
        in_specs=[
            pl.BlockSpec((ATT_BLOCK, pair), lambda b, h, i: (b * nq + i, COL_Q // pair + h)),
            pl.BlockSpec((t_pad, pair), lambda b, h, i: (b, COL_K // pair + h)),
            pl.BlockSpec((t_pad, pair), lambda b, h, i: (b, COL_V // pair + h)),
            pl.BlockSpec((ATT_BLOCK, ATT_BLOCK), lambda b, h, i: (0, 0)),
        ],
        out_specs=pl.BlockSpec((ATT_BLOCK, pair), lambda b, h, i: (b * nq + i, h)),
        out_shape=jax.ShapeDtypeStruct((m, D_ATTN), F32),
        scratch_shapes=[pltpu.VMEM((2, ATT_BLOCK, pair), F32), pltpu.VMEM((2, ATT_BLOCK, 1), F32)],
        compiler_params=_params(("parallel", "parallel", "arbitrary")),
        name="attn_prompt",
    )(p, p, p, tri)


def _attn_sample_kernel(q_ref, kn_ref, vn_ref, kc_ref, vc_ref, tri_ref, oin_ref, o_ref, *, dec_seq, past_len):
    del oin_ref
    blk = ATT_BLOCK
    rows = N_HEADS * dec_seq
    q = q_ref[...] * (HEAD_DIM ** -0.5)
    lane_head = lax.broadcasted_iota(jnp.int32, (dec_seq, D_ATTN), 1) // HEAD_DIM
    qs = jnp.concatenate([jnp.where(lane_head == h, q, 0.0) for h in range(N_HEADS)], axis=0).astype(BF16)
    tri = tri_ref[...]

    pad = jnp.zeros((blk - dec_seq, D_ATTN), F32)
    k_new = jnp.concatenate([kn_ref[...], pad], axis=0).astype(BF16)
    v_new = jnp.concatenate([vn_ref[...], pad], axis=0).astype(BF16)
    tq = lax.broadcasted_iota(jnp.int32, (rows, blk), 0) % dec_seq
    col = lax.broadcasted_iota(jnp.int32, (rows, blk), 1)
    acc, car = _stick_block(qs, k_new, v_new, tri, jnp.zeros((rows, 1), F32), col < tq)
    for j in reversed(range(past_len // blk)):
        kblk = kc_ref[0, pl.ds(j * blk, blk), :].astype(BF16)
        vblk = vc_ref[0, pl.ds(j * blk, blk), :].astype(BF16)
        out, car = _stick_block(qs, kblk, vblk, tri, car, None)
        acc = acc + out
    o = jnp.zeros((dec_seq, D_ATTN), F32)
    for h in range(N_HEADS):
        o = o + jnp.where(lane_head == h, acc[h * dec_seq:(h + 1) * dec_seq, :], 0.0)
    o_ref[...] = o


def _attn_sample(p, o_all, cache_k, cache_v, tri, row0, n_streams, dec_seq):
    past_len = cache_k.shape[1]
    assert past_len % ATT_BLOCK == 0 and row0 % dec_seq == 0 and dec_seq % 8 == 0
    blk0 = row0 // dec_seq
    kern = functools.partial(_attn_sample_kernel, dec_seq=dec_seq, past_len=past_len)
    return pl.pallas_call(
        kern,
        grid=(n_streams,),
        in_specs=[
            pl.BlockSpec((dec_seq, D_ATTN), lambda b: (blk0 + b, COL_Q // D_ATTN)),
            pl.BlockSpec((dec_seq, D_ATTN), lambda b: (blk0 + b, COL_K // D_ATTN)),
            pl.BlockSpec((dec_seq, D_ATTN), lambda b: (blk0 + b, COL_V // D_ATTN)),
            pl.BlockSpec((1, past_len, D_ATTN), lambda b: (b, 0, 0)),
            pl.BlockSpec((1, past_len, D_ATTN), lambda b: (b, 0, 0)),
            pl.BlockSpec((ATT_BLOCK, ATT_BLOCK), lambda b: (0, 0)),
            pl.BlockSpec(memory_space=pl.ANY),
        ],
        out_specs=pl.BlockSpec((dec_seq, D_ATTN), lambda b: (blk0 + b, 0)),
        out_shape=jax.ShapeDtypeStruct(o_all.shape, F32),
        input_output_aliases={6: 0},
        compiler_params=_params(("parallel",)),
        name="attn_sample",
    )(p, p, p, cache_k, cache_v, tri, o_all)


def _seqmix_kernel(main_ref, halo_ref, cw_ref, pm_ref, cv_ref, cx_ref, *, pos_offset):
    i = pl.program_id(1)
    tm = main_ref.shape[0]
    main = main_ref[...]
    halo = jnp.where(i == 0, 0.0, halo_ref[...])
    ext = jnp.concatenate([halo, main], axis=0)
    a = ext[:, 0:D_POOL]
    cx = ext[:, D_POOL + 2 * D_CONV:D_POOL + 3 * D_CONV] * ext[:, D_POOL:D_POOL + D_CONV]

    sums = []
    s = a
    for w in POOL_WINDOWS:
        s = s + pltpu.roll(s, w // 2, 0)
        sums.append(s)
    pos = i * tm + pos_offset + lax.broadcasted_iota(jnp.int32, (HALO + tm, 1), 0) - HALO
    lane_group = lax.broadcasted_iota(jnp.int32, (HALO + tm, D_POOL), 1) // POOL_GROUP
    mean = jnp.zeros_like(a)
    for g, w in enumerate(POOL_WINDOWS):
        cnt = jnp.minimum(pos + 1, w).astype(F32)
        mean = jnp.where(lane_group == g, sums[g] / cnt, mean)
    pm_ref[...] = (mean - a)[HALO:]

    cw = cw_ref[...]
    y = pltpu.roll(cx, 2, 0) * cw[0:1] + pltpu.roll(cx, 1, 0) * cw[1:2] + cx * cw[2:3]
    cv_ref[...] = main[:, D_POOL + D_CONV:D_POOL + 2 * D_CONV] * y[HALO:]
    cx_ref[...] = cx[HALO:]


def _seqmix(p, conv_w, col_block, n_batch, t_rows, tm, pos_offset):
    nt = t_rows // tm
    per = tm // HALO
    kern = functools.partial(_seqmix_kernel, pos_offset=pos_offset)
    out = jax.ShapeDtypeStruct((n_batch * t_rows, D_POOL), F32)
    return pl.pallas_call(
        kern,
        grid=(n_batch, nt),
        in_specs=[
            pl.BlockSpec((tm, 1024), lambda b, i: (b * nt + i, col_block)),
            pl.BlockSpec((HALO, 1024), lambda b, i: (jnp.maximum((b * nt + i) * per - 1, 0), col_block)),
            pl.BlockSpec((8, D_CONV), lambda b, i: (0, 0)),
        ],
        out_specs=[pl.BlockSpec((tm, D_POOL), lambda b, i: (b * nt + i, 0))] * 3,
        out_shape=[out] * 3,
        compiler_params=_params(("parallel", "arbitrary")),
        name="seqmix",
    )(p, p, conv_w)


def _mixout_kernel(h_ref, g_ref, pm_ref, cv_ref, o_ref, wgrp_ref, ps_ref, pp_ref, cp_ref, ap_ref, wo_ref, out_ref):
    ya = jnp.dot(pm_ref[...].astype(BF16), wgrp_ref[...], preferred_element_type=F32) * ps_ref[...]
    ya = jnp.dot(ya.astype(BF16), pp_ref[...], preferred_element_type=F32)
    yb = jnp.dot(cv_ref[...].astype(BF16), cp_ref[...], preferred_element_type=F32)
    yc = jnp.dot(o_ref[...].astype(BF16), ap_ref[...], preferred_element_type=F32)
    mixed = (g_ref[:, 0:D_MODEL] * ya + g_ref[:, D_MODEL:2 * D_MODEL] * yb
             + g_ref[:, 2 * D_MODEL:3 * D_MODEL] * yc)
    out_ref[...] = h_ref[...] + jnp.dot(mixed.astype(BF16), wo_ref[...], preferred_element_type=F32)


def _mixout(h, p, pm, cv, o, wgrp, pscale, pool_proj, conv_proj, attn_proj, w_out):
    m = h.shape[0]
    row = lambda i: (i, 0)
    const = lambda i: (0, 0)
    return pl.pallas_call(
        _mixout_kernel,
        grid=(m // TM,),
        in_specs=[
            pl.BlockSpec((TM, D_MODEL), row),
            pl.BlockSpec((TM, N_GATE), row),
            pl.BlockSpec((TM, D_POOL), row),
            pl.BlockSpec((TM, D_CONV), row),
            pl.BlockSpec((TM, D_ATTN), row),
            pl.BlockSpec((D_POOL, D_POOL), const),
            pl.BlockSpec((1, D_POOL), const),
            pl.BlockSpec((D_POOL, D_MODEL), const),
            pl.BlockSpec((D_CONV, D_MODEL), const),
            pl.BlockSpec((D_ATTN, D_MODEL), const),
            pl.BlockSpec((D_MODEL, D_MODEL), const),
        ],
        out_specs=pl.BlockSpec((TM, D_MODEL), row),
        out_shape=jax.ShapeDtypeStruct((m, D_MODEL), F32),
        compiler_params=_params(("parallel",)),
        name="mixout",
    )(h, p, pm, cv, o, wgrp, pscale, pool_proj, conv_proj, attn_proj, w_out)


def _block_diag(w):
    g, c, d = w.shape
    eye = jnp.eye(g, dtype=w.dtype)
    return (eye[:, None, :, None] * w[:, :, None, :]).reshape(g * c, g * d)


def kernel(x_prompt, x_sample, cache_k, cache_v, state_pool, state_conv, meta, ffn1_norm, ffn1_w_gate, ffn1_w_up, ffn1_w_down, mix_norm, w_in, pool_w, pool_scale, pool_proj, conv_w, conv_proj, q_norm, k_norm, attn_proj, w_out, ffn2_norm, ffn2_w_gate, ffn2_w_up, ffn2_w_down):
    n_batch, seq, _ = x_prompt.shape
    n_streams, dec_seq, _ = x_sample.shape
    depth = w_in.shape[0]
    past_len = cache_k.shape[2]
    t_real = N_META + seq
    t_pad = -(-t_real // ATT_BLOCK) * ATT_BLOCK
    rows_p = n_batch * t_pad
    rows_s = n_streams * dec_seq
    m_rows = rows_p + rows_s
    assert m_rows % TM == 0 and dec_seq > POOL_STATE and (HALO + dec_seq) % 8 == 0

    meta_b = jnp.broadcast_to(meta[None].astype(F32), (n_batch, N_META, D_MODEL))
    tail = jnp.zeros((n_batch, t_pad - t_real, D_MODEL), F32)
    hp = jnp.concatenate([meta_b, x_prompt, tail], axis=1).reshape(rows_p, D_MODEL)
    h = jnp.concatenate([hp, x_sample.reshape(rows_s, D_MODEL)], axis=0)

    n_seq_cols = D_POOL + 3 * D_CONV + 3 * D_ATTN
    perm = np.concatenate([np.arange(n_seq_cols, n_seq_cols + N_GATE), np.arange(n_seq_cols)])
    tri = (np.arange(ATT_BLOCK)[:, None] >= np.arange(ATT_BLOCK)[None, :])
    tri = jnp.asarray(tri, BF16)
    seg = jnp.asarray(np.kron(np.eye(N_HEADS), np.full((HEAD_DIM, HEAD_DIM), 1.0 / HEAD_DIM)), BF16)

    outs = [[] for _ in range(8)]
    for l in range(depth):
        h = _ffn(h, ffn1_norm[l][None], ffn1_w_gate[l].astype(BF16), ffn1_w_up[l].astype(BF16),
                 ffn1_w_down[l].astype(BF16))
        p = _proj(h, mix_norm[l][None], w_in[l][:, perm].astype(BF16), q_norm[l].reshape(1, D_ATTN),
                  k_norm[l].reshape(1, D_ATTN), seg)

        o = _attn_prompt(p, tri, n_batch, t_pad)
        o = _attn_sample(p, o, cache_k[l].reshape(n_streams, past_len, D_ATTN),
                         cache_v[l].reshape(n_streams, past_len, D_ATTN), tri, rows_p, n_streams, dec_seq)

        cw = jnp.zeros((8, D_CONV), F32).at[:CONV_W].set(conv_w[l])
        pm_p, cv_p, cx_p = _seqmix(p, cw, COL_SEQ // 1024, n_batch, t_pad, ATT_BLOCK, 0)
        seq_s = p[rows_p:, COL_SEQ:COL_SEQ + 1024].reshape(n_streams, dec_seq, 1024)
        hist = jnp.zeros((n_streams, HALO, 1024), F32)
        hist = hist.at[:, HALO - POOL_STATE:, 0:D_POOL].set(state_pool[l])
        hist = hist.at[:, HALO - (CONV_W - 1):, D_POOL:D_POOL + D_CONV].set(state_conv[l])
        hist = hist.at[:, :, D_POOL + 2 * D_CONV:].set(1.0)
        ext_s = jnp.concatenate([hist, seq_s], axis=1).reshape(n_streams * (HALO + dec_seq), 1024)
        pm_s, cv_s, cx_s = _seqmix(ext_s, cw, 0, 1, ext_s.shape[0], ext_s.shape[0], past_len)
        data = lambda t: t.reshape(n_streams, HALO + dec_seq, -1)[:, HALO:]
        pm = jnp.concatenate([pm_p, data(pm_s).reshape(rows_s, D_POOL)], axis=0)
        cv = jnp.concatenate([cv_p, data(cv_s).reshape(rows_s, D_CONV)], axis=0)

        h = _mixout(h, p, pm, cv, o, _block_diag(pool_w[l]).astype(BF16), pool_scale[l][None],
                    pool_proj[l].astype(BF16), conv_proj[l].astype(BF16), attn_proj[l].astype(BF16),
                    w_out[l].astype(BF16))
        h = _ffn(h, ffn2_norm[l][None], ffn2_w_gate[l].astype(BF16), ffn2_w_up[l].astype(BF16),
                 ffn2_w_down[l].astype(BF16))

        pp3 = p[:rows_p].reshape(n_batch, t_pad, IN_COLS)
        ps3 = p[rows_p:].reshape(n_streams, dec_seq, IN_COLS)
        heads = lambda t: t.reshape(t.shape[0], t.shape[1], N_HEADS, HEAD_DIM)
        outs[0].append(heads(pp3[:, :t_real, COL_K:COL_K + D_ATTN]))
        outs[1].append(heads(pp3[:, :t_real, COL_V:COL_V + D_ATTN]))
        outs[2].append(pp3[:, t_real - POOL_STATE:t_real, COL_SEQ:COL_SEQ + D_POOL])
        outs[3].append(cx_p.reshape(n_batch, t_pad, D_CONV)[:, t_real - (CONV_W - 1):t_real])
        outs[4].append(heads(ps3[:, :, COL_K:COL_K + D_ATTN]))
        outs[5].append(heads(ps3[:, :, COL_V:COL_V + D_ATTN]))
        outs[6].append(ps3[:, dec_seq - POOL_STATE:, COL_SEQ:COL_SEQ + D_POOL])
        outs[7].append(data(cx_s)[:, dec_seq - (CONV_W - 1):])

    y_prompt = h[:rows_p].reshape(n_batch, t_pad, D_MODEL)[:, N_META:t_real]
    y_sample = h[rows_p:].reshape(n_streams, dec_seq, D_MODEL)
    return (y_prompt, y_sample) + tuple(jnp.stack(o) for o in outs)
```

```python
import functools

import jax
import jax.numpy as jnp
import numpy as np
from jax import lax
from jax.experimental import pallas as pl
from jax.experimental.pallas import tpu as pltpu

F32 = jnp.float32
BF16 = jnp.bfloat16

D_MODEL = 1024
N_META = 16
D_POOL = 256
POOL_WINDOWS = (2, 4, 8, 16)
POOL_GROUP = 64
POOL_STATE = 15
D_CONV = 256
CONV_W = 3
N_HEADS = 8
HEAD_DIM = 64
D_ATTN = N_HEADS * HEAD_DIM
D_FF = 2816
RMS_EPS = 1e-6
LOG2E = 1.4426950408889634
Q_SCALE = HEAD_DIM ** -0.5 * LOG2E
N_GATE = 3 * D_MODEL
D_SEQ = D_POOL + 3 * D_CONV

ATT_BLOCK = 256
ATT_HEADS = 4
HALO = 16
TM = 512
PROJ_CHUNK = 512
TF = 1408
VMEM_LIMIT = 56 * 1024 * 1024


def _params(sem):
    return pltpu.CompilerParams(dimension_semantics=sem, vmem_limit_bytes=VMEM_LIMIT)


def _rms_rows(x, gain):
    ms = jnp.mean(x * x, axis=-1, keepdims=True)
    return x * lax.rsqrt(ms + RMS_EPS) * gain


def _ffn_kernel(x_ref, g_ref, wg_ref, wu_ref, wd_ref, o_ref, xn_ref, acc_ref):
    j = pl.program_id(1)

    @pl.when(j == 0)
    def _():
        xn_ref[...] = _rms_rows(x_ref[...], g_ref[...]).astype(BF16)
        acc_ref[...] = jnp.zeros_like(acc_ref)

    xn = xn_ref[...]
    g = jnp.dot(xn, wg_ref[...], preferred_element_type=F32)
    u = jnp.dot(xn, wu_ref[...], preferred_element_type=F32)
    a = (g * jax.nn.sigmoid(g) * u).astype(BF16)
    acc_ref[...] += jnp.dot(a, wd_ref[...], preferred_element_type=F32)

    @pl.when(j == pl.num_programs(1) - 1)
    def _():
        o_ref[...] = x_ref[...] + 0.5 * acc_ref[...]


def _ffn(x, gain, wg, wu, wd):
    m = x.shape[0]
    return pl.pallas_call(
        _ffn_kernel,
        grid=(m // TM, D_FF // TF),
        in_specs=[
            pl.BlockSpec((TM, D_MODEL), lambda i, j: (i, 0)),
            pl.BlockSpec((1, D_MODEL), lambda i, j: (0, 0)),
            pl.BlockSpec((D_MODEL, TF), lambda i, j: (0, j)),
            pl.BlockSpec((D_MODEL, TF), lambda i, j: (0, j)),
            pl.BlockSpec((TF, D_MODEL), lambda i, j: (j, 0)),
        ],
        out_specs=pl.BlockSpec((TM, D_MODEL), lambda i, j: (i, 0)),
        out_shape=jax.ShapeDtypeStruct((m, D_MODEL), F32),
        scratch_shapes=[pltpu.VMEM((TM, D_MODEL), BF16), pltpu.VMEM((TM, D_MODEL), F32)],
        compiler_params=_params(("parallel", "arbitrary")),
        name="ffn",
    )(x, gain, wg, wu, wd)


def _head_rms(p, seg, gain):
    pp = p * p
    hi = pp.astype(BF16)
    lo = (pp - hi.astype(F32)).astype(BF16)
    ms = jnp.dot(hi, seg, preferred_element_type=F32) + jnp.dot(lo, seg, preferred_element_type=F32)
    return p * lax.rsqrt(ms + RMS_EPS) * gain


def _proj_kernel(h_ref, g_ref, wg_ref, ws_ref, wq_ref, wk_ref, wv_ref, qn_ref, kn_ref, seg_ref,
                 gate_ref, seq_ref, q_ref, k_ref, kb_ref, v_ref, vb_ref):
    u = _rms_rows(h_ref[...], g_ref[...]).astype(BF16)
    ch = PROJ_CHUNK
    for c in range(N_GATE // ch):
        p = jnp.dot(u, wg_ref[:, c * ch:(c + 1) * ch], preferred_element_type=F32)
        gate_ref[:, c * ch:(c + 1) * ch] = jax.nn.sigmoid(p).astype(BF16)
    for c in range(D_SEQ // ch):
        seq_ref[:, c * ch:(c + 1) * ch] = jnp.dot(u, ws_ref[:, c * ch:(c + 1) * ch], preferred_element_type=F32)
    q = _head_rms(jnp.dot(u, wq_ref[...], preferred_element_type=F32), seg_ref[...], qn_ref[...])
    q_ref[...] = (q * Q_SCALE).astype(BF16)
    k = _head_rms(jnp.dot(u, wk_ref[...], preferred_element_type=F32), seg_ref[...], kn_ref[...])
    k_ref[...] = k
    kb_ref[...] = k.astype(BF16)
    v = jnp.dot(u, wv_ref[...], preferred_element_type=F32)
    v_ref[...] = v
    vb_ref[...] = v.astype(BF16)


def _proj(h, gain, wg, ws, wq, wk, wv, qn, kn, seg):
    m = h.shape[0]
    row = lambda i: (i, 0)
    const = lambda i: (0, 0)
    resident = lambda shape: pl.BlockSpec(shape, const, pipeline_mode=pl.Buffered(1))
    widths = [(N_GATE, BF16), (D_SEQ, F32), (D_ATTN, BF16), (D_ATTN, F32), (D_ATTN, BF16), (D_ATTN, F32), (D_ATTN, BF16)]
    return pl.pallas_call(
        _proj_kernel,
        grid=(m // TM,),
        in_specs=[
            pl.BlockSpec((TM, D_MODEL), row),
            pl.BlockSpec((1, D_MODEL), const),
            resident((D_MODEL, N_GATE)),
            resident((D_MODEL, D_SEQ)),
            resident((D_MODEL, D_ATTN)),
            resident((D_MODEL, D_ATTN)),
            resident((D_MODEL, D_ATTN)),
            pl.BlockSpec((1, D_ATTN), const),
            pl.BlockSpec((1, D_ATTN), const),
            resident((D_ATTN, D_ATTN)),
        ],
        out_specs=[pl.BlockSpec((TM, w), row) for w, _ in widths],
        out_shape=[jax.ShapeDtypeStruct((m, w), dt) for w, dt in widths],
        compiler_params=_params(("parallel",)),
        name="proj",
    )(h, gain, wg, ws, wq, wk, wv, qn, kn, seg)


def _scores(qs, kblk):
    return [lax.dot_general(q, kblk, (((1,), (1,)), ((), ())), preferred_element_type=F32) for q in qs]


def _stick_probs(zs, tri2, carries, valid):
    parts = []
    for z in zs:
        neg_abs = lax.bitcast_convert_type(lax.bitcast_convert_type(z, jnp.uint32) | jnp.uint32(0x80000000), F32)
        sp = jnp.maximum(z, 0.0) + jnp.log(1.0 + jnp.exp2(neg_abs)) * LOG2E
        if valid is not None:
            sp = jnp.where(valid, sp, 0.0)
        hi = sp.astype(BF16)
        parts.append(jnp.concatenate([hi, (sp - hi.astype(F32)).astype(BF16)], axis=1))
    cums = [jnp.dot(hl, tri2, preferred_element_type=F32) for hl in parts]
    probs = []
    for z, cum, carry in zip(zs, cums, carries):
        a = jnp.exp2(z - cum - carry)
        if valid is not None:
            a = jnp.where(valid, a, 0.0)
        probs.append(a.astype(BF16))
    return probs, [carry + cum[:, 0:1] for carry, cum in zip(carries, cums)]


def _attn_prompt_kernel(q_ref, k_ref, v_ref, tri_ref, o_ref, acc_ref, car_ref, z_ref, p_ref):
    i = pl.program_id(2)
    blk = ATT_BLOCK
    nh = ATT_HEADS
    lane_head = lax.broadcasted_iota(jnp.int32, (blk, nh * HEAD_DIM), 1) // HEAD_DIM
    q2 = q_ref[...]
    qs = [jnp.where(lane_head == h, q2, jnp.zeros_like(q2)) for h in range(nh)]
    tri2 = tri_ref[...]

    def scores(j):
        kblk = k_ref[pl.ds(pl.multiple_of(j * blk, blk), blk), :]
        return _scores(qs, kblk)

    def weighted_values(j):
        vblk = v_ref[pl.ds(pl.multiple_of(j * blk, blk), blk), :]
        vstack = jnp.concatenate([jnp.where(lane_head == h, vblk, jnp.zeros_like(vblk)) for h in range(nh)], axis=0)
        return jnp.dot(p_ref[...], vstack, preferred_element_type=F32)

    def weights(valid):
        probs, cars = _stick_probs([z_ref[h] for h in range(nh)], tri2, [car_ref[h] for h in range(nh)], valid)
        for h in range(nh):
            p_ref[:, h * blk:(h + 1) * blk] = probs[h]
            car_ref[h] = cars[h]

    def store_scores(zs):
        for h in range(nh):
            z_ref[h] = zs[h]

    car_ref[...] = jnp.zeros_like(car_ref)
    store_scores(scores(i))
    z_next = scores(jnp.maximum(i - 1, 0))
    row = lax.broadcasted_iota(jnp.int32, (blk, blk), 0)
    col = lax.broadcasted_iota(jnp.int32, (blk, blk), 1)
    weights(col < row)
    store_scores(z_next)
    acc_ref[...] = jnp.zeros_like(acc_ref)

    def body(t, c):
        j = i - t
        acc_ref[...] += weighted_values(j + 1)
        z_next = scores(jnp.maximum(j - 1, 0))
        weights(None)
        store_scores(z_next)
        return c

    lax.fori_loop(1, i + 1, body, 0)
    o_ref[...] = acc_ref[...] + weighted_values(0)


def _attn_prompt(q, kb, vb, tri, n_batch, t_pad):
    m = q.shape[0]
    nq = t_pad // ATT_BLOCK
    pair = ATT_HEADS * HEAD_DIM
    return pl.pallas_call(
        _attn_prompt_kernel,
        grid=(n_batch, N_HEADS // ATT_HEADS, nq),
        in_specs=[
            pl.BlockSpec((ATT_BLOCK, pair), lambda b, h, i: (b * nq + i, h)),
            pl.BlockSpec((t_pad, pair), lambda b, h, i: (b, h)),
            pl.BlockSpec((t_pad, pair), lambda b, h, i: (b, h)),
            pl.BlockSpec((2 * ATT_BLOCK, ATT_BLOCK), lambda b, h, i: (0, 0)),
        ],
        out_specs=pl.BlockSpec((ATT_BLOCK, pair), lambda b, h, i: (b * nq + i, h)),
        out_shape=jax.ShapeDtypeStruct((m, D_ATTN), F32),
        scratch_shapes=[pltpu.VMEM((ATT_BLOCK, pair), F32), pltpu.VMEM((ATT_HEADS, ATT_BLOCK, 1), F32),
                        pltpu.VMEM((ATT_HEADS, ATT_BLOCK, ATT_BLOCK), F32),
                        pltpu.VMEM((ATT_BLOCK, ATT_HEADS * ATT_BLOCK), BF16)],
        compiler_params=_params(("parallel", "parallel", "arbitrary")),
        name="attn_prompt",
    )(q, kb, vb, tri)


def _attn_sample_kernel(q_ref, kn_ref, vn_ref, kc_ref, vc_ref, tri_ref, oin_ref, o_ref, *, dec_seq, past_len):
    del oin_ref
    blk = ATT_BLOCK
    rows = N_HEADS * dec_seq
    q = q_ref[...]
    lane_head = lax.broadcasted_iota(jnp.int32, (dec_seq, D_ATTN), 1) // HEAD_DIM
    qs = jnp.concatenate([jnp.where(lane_head == h, q, jnp.zeros_like(q)) for h in range(N_HEADS)], axis=0)
    tri2 = tri_ref[...]

    pad = jnp.zeros((blk - dec_seq, D_ATTN), BF16)
    k_new = jnp.concatenate([kn_ref[...], pad], axis=0)
    v_new = jnp.concatenate([vn_ref[...], pad], axis=0)
    tq = lax.broadcasted_iota(jnp.int32, (rows, blk), 0) % dec_seq
    col = lax.broadcasted_iota(jnp.int32, (rows, blk), 1)
    (a,), (car,) = _stick_probs(_scores([qs], k_new), tri2, [jnp.zeros((rows, 1), F32)], col < tq)
    acc = jnp.dot(a, v_new, preferred_element_type=F32)
    for j in reversed(range(past_len // blk)):
        kblk = kc_ref[0, pl.ds(j * blk, blk), :].astype(BF16)
        vblk = vc_ref[0, pl.ds(j * blk, blk), :].astype(BF16)
        (a,), (car,) = _stick_probs(_scores([qs], kblk), tri2, [car], None)
        acc = acc + jnp.dot(a, vblk, preferred_element_type=F32)
    o = jnp.zeros((dec_seq, D_ATTN), F32)
    for h in range(N_HEADS):
        o = o + jnp.where(lane_head == h, acc[h * dec_seq:(h + 1) * dec_seq, :], 0.0)
    o_ref[...] = o


def _attn_sample(q, kb, vb, o_all, cache_k, cache_v, tri, row0, n_streams, dec_seq):
    past_len = cache_k.shape[1]
    assert past_len % ATT_BLOCK == 0 and row0 % dec_seq == 0 and dec_seq % 16 == 0
    blk0 = row0 // dec_seq
    kern = functools.partial(_attn_sample_kernel, dec_seq=dec_seq, past_len=past_len)
    return pl.pallas_call(
        kern,
        grid=(n_streams,),
        in_specs=[
            pl.BlockSpec((dec_seq, D_ATTN), lambda b: (blk0 + b, 0)),
            pl.BlockSpec((dec_seq, D_ATTN), lambda b: (blk0 + b, 0)),
            pl.BlockSpec((dec_seq, D_ATTN), lambda b: (blk0 + b, 0)),
            pl.BlockSpec((1, past_len, D_ATTN), lambda b: (b, 0, 0)),
            pl.BlockSpec((1, past_len, D_ATTN), lambda b: (b, 0, 0)),
            pl.BlockSpec((2 * ATT_BLOCK, ATT_BLOCK), lambda b: (0, 0)),
            pl.BlockSpec(memory_space=pl.ANY),
        ],
        out_specs=pl.BlockSpec((dec_seq, D_ATTN), lambda b: (blk0 + b, 0)),
        out_shape=jax.ShapeDtypeStruct(o_all.shape, F32),
        input_output_aliases={6: 0},
        compiler_params=_params(("parallel",)),
        name="attn_sample",
    )(q, kb, vb, cache_k, cache_v, tri, o_all)


def _seqmix_kernel(main_ref, halo_ref, cw_ref, pm_ref, cv_ref, cx_ref, *, pos_offset):
    i = pl.program_id(1)
    tm = main_ref.shape[0]
    main = main_ref[...]
    halo = jnp.where(i == 0, 0.0, halo_ref[...])
    ext = jnp.concatenate([halo, main], axis=0)
    a = ext[:, 0:D_POOL]
    cx = ext[:, D_POOL + 2 * D_CONV:D_POOL + 3 * D_CONV] * ext[:, D_POOL:D_POOL + D_CONV]

    sums = []
    s = a
    for w in POOL_WINDOWS:
        s = s + pltpu.roll(s, w // 2, 0)
        sums.append(s)
    pos = i * tm + pos_offset + lax.broadcasted_iota(jnp.int32, (HALO + tm, 1), 0) - HALO
    lane_group = lax.broadcasted_iota(jnp.int32, (HALO + tm, D_POOL), 1) // POOL_GROUP
    mean = jnp.zeros_like(a)
    for g, w in enumerate(POOL_WINDOWS):
        cnt = jnp.minimum(pos + 1, w).astype(F32)
        mean = jnp.where(lane_group == g, sums[g] / cnt, mean)
    pm_ref[...] = (mean - a)[HALO:]

    cw = cw_ref[...]
    y = pltpu.roll(cx, 2, 0) * cw[0:1] + pltpu.roll(cx, 1, 0) * cw[1:2] + cx * cw[2:3]
    cv_ref[...] = main[:, D_POOL + D_CONV:D_POOL + 2 * D_CONV] * y[HALO:]
    cx_ref[...] = cx[HALO:]


def _seqmix(p, conv_w, n_batch, t_rows, tm, pos_offset):
    nt = t_rows // tm
    per = tm // HALO
    kern = functools.partial(_seqmix_kernel, pos_offset=pos_offset)
    out = jax.ShapeDtypeStruct((n_batch * t_rows, D_POOL), F32)
    return pl.pallas_call(
        kern,
        grid=(n_batch, nt),
        in_specs=[
            pl.BlockSpec((tm, D_SEQ), lambda b, i: (b * nt + i, 0)),
            pl.BlockSpec((HALO, D_SEQ), lambda b, i: (jnp.maximum((b * nt + i) * per - 1, 0), 0)),
            pl.BlockSpec((8, D_CONV), lambda b, i: (0, 0)),
        ],
        out_specs=[pl.BlockSpec((tm, D_POOL), lambda b, i: (b * nt + i, 0))] * 3,
        out_shape=[out] * 3,
        compiler_params=_params(("parallel", "arbitrary")),
        name="seqmix",
    )(p, p, conv_w)


def _mixout_kernel(h_ref, g_ref, pm_ref, cv_ref, o_ref, wgrp_ref, ps_ref, pp_ref, cp_ref, ap_ref, wo_ref, out_ref):
    ya = jnp.dot(pm_ref[...].astype(BF16), wgrp_ref[...], preferred_element_type=F32) * ps_ref[...]
    ya = jnp.dot(ya.astype(BF16), pp_ref[...], preferred_element_type=F32)
    yb = jnp.dot(cv_ref[...].astype(BF16), cp_ref[...], preferred_element_type=F32)
    yc = jnp.dot(o_ref[...].astype(BF16), ap_ref[...], preferred_element_type=F32)
    g = g_ref[...].astype(F32)
    mixed = g[:, 0:D_MODEL] * ya + g[:, D_MODEL:2 * D_MODEL] * yb + g[:, 2 * D_MODEL:3 * D_MODEL] * yc
    out_ref[...] = h_ref[...] + jnp.dot(mixed.astype(BF16), wo_ref[...], preferred_element_type=F32)


def _mixout(h, gates, pm, cv, o, wgrp, pscale, pool_proj, conv_proj, attn_proj, w_out):
    m = h.shape[0]
    row = lambda i: (i, 0)
    const = lambda i: (0, 0)
    return pl.pallas_call(
        _mixout_kernel,
        grid=(m // TM,),
        in_specs=[
            pl.BlockSpec((TM, D_MODEL), row),
            pl.BlockSpec((TM, N_GATE), row),
            pl.BlockSpec((TM, D_POOL), row),
            pl.BlockSpec((TM, D_CONV), row),
            pl.BlockSpec((TM, D_ATTN), row),
            pl.BlockSpec((D_POOL, D_POOL), const),
            pl.BlockSpec((1, D_POOL), const),
            pl.BlockSpec((D_POOL, D_MODEL), const),
            pl.BlockSpec((D_CONV, D_MODEL), const),
            pl.BlockSpec((D_ATTN, D_MODEL), const),
            pl.BlockSpec((D_MODEL, D_MODEL), const),
        ],
        out_specs=pl.BlockSpec((TM, D_MODEL), row),
        out_shape=jax.ShapeDtypeStruct((m, D_MODEL), F32),
        compiler_params=_params(("parallel",)),
        name="mixout",
    )(h, gates, pm, cv, o, wgrp, pscale, pool_proj, conv_proj, attn_proj, w_out)


def _block_diag(w):
    g, c, d = w.shape
    eye = jnp.eye(g, dtype=w.dtype)
    return (eye[:, None, :, None] * w[:, :, None, :]).reshape(g * c, g * d)


def kernel(x_prompt, x_sample, cache_k, cache_v, state_pool, state_conv, meta, ffn1_norm, ffn1_w_gate, ffn1_w_up, ffn1_w_down, mix_norm, w_in, pool_w, pool_scale, pool_proj, conv_w, conv_proj, q_norm, k_norm, attn_proj, w_out, ffn2_norm, ffn2_w_gate, ffn2_w_up, ffn2_w_down):
    n_batch, seq, _ = x_prompt.shape
    n_streams, dec_seq, _ = x_sample.shape
    depth = w_in.shape[0]
    past_len = cache_k.shape[2]
    t_real = N_META + seq
    t_pad = -(-t_real // ATT_BLOCK) * ATT_BLOCK
    rows_p = n_batch * t_pad
    rows_s = n_streams * dec_seq
    m_rows = rows_p + rows_s
    assert m_rows % TM == 0 and dec_seq > POOL_STATE and (HALO + dec_seq) % 8 == 0

    meta_b = jnp.broadcast_to(meta[None].astype(F32), (n_batch, N_META, D_MODEL))
    tail = jnp.zeros((n_batch, t_pad - t_real, D_MODEL), F32)
    hp = jnp.concatenate([meta_b, x_prompt, tail], axis=1).reshape(rows_p, D_MODEL)
    h = jnp.concatenate([hp, x_sample.reshape(rows_s, D_MODEL)], axis=0)

    c_q = D_SEQ
    c_k, c_v, c_g = c_q + D_ATTN, c_q + 2 * D_ATTN, c_q + 3 * D_ATTN
    tri = (np.arange(ATT_BLOCK)[:, None] >= np.arange(ATT_BLOCK)[None, :])
    tri = jnp.asarray(np.concatenate([tri, tri], axis=0), BF16)
    seg = jnp.asarray(np.kron(np.eye(N_HEADS), np.full((HEAD_DIM, HEAD_DIM), 1.0 / HEAD_DIM)), BF16)

    outs = [[] for _ in range(8)]
    for l in range(depth):
        h = _ffn(h, ffn1_norm[l][None], ffn1_w_gate[l].astype(BF16), ffn1_w_up[l].astype(BF16),
                 ffn1_w_down[l].astype(BF16))
        w = w_in[l].astype(BF16)
        gates, sq, q, k, kb, v, vb = _proj(
            h, mix_norm[l][None], w[:, c_g:], w[:, :c_q], w[:, c_q:c_k], w[:, c_k:c_v], w[:, c_v:c_g],
            q_norm[l].reshape(1, D_ATTN), k_norm[l].reshape(1, D_ATTN), seg)

        o = _attn_prompt(q, kb, vb, tri, n_batch, t_pad)
        o = _attn_sample(q, kb, vb, o, cache_k[l].reshape(n_streams, past_len, D_ATTN),
                         cache_v[l].reshape(n_streams, past_len, D_ATTN), tri, rows_p, n_streams, dec_seq)

        cw = jnp.zeros((8, D_CONV), F32).at[:CONV_W].set(conv_w[l])
        pm_p, cv_p, cx_p = _seqmix(sq, cw, n_batch, t_pad, ATT_BLOCK, 0)
        seq_s = sq[rows_p:].reshape(n_streams, dec_seq, D_SEQ)
        hist = jnp.zeros((n_streams, HALO, D_SEQ), F32)
        hist = hist.at[:, HALO - POOL_STATE:, 0:D_POOL].set(state_pool[l])
        hist = hist.at[:, HALO - (CONV_W - 1):, D_POOL:D_POOL + D_CONV].set(state_conv[l])
        hist = hist.at[:, :, D_POOL + 2 * D_CONV:].set(1.0)
        ext_s = jnp.concatenate([hist, seq_s], axis=1).reshape(n_streams * (HALO + dec_seq), D_SEQ)
        pm_s, cv_s, cx_s = _seqmix(ext_s, cw, 1, ext_s.shape[0], ext_s.shape[0], past_len)
        data = lambda t: t.reshape(n_streams, HALO + dec_seq, -1)[:, HALO:]
        pm = jnp.concatenate([pm_p, data(pm_s).reshape(rows_s, D_POOL)], axis=0)
        cv = jnp.concatenate([cv_p, data(cv_s).reshape(rows_s, D_CONV)], axis=0)

        h = _mixout(h, gates, pm, cv, o, _block_diag(pool_w[l]).astype(BF16), pool_scale[l][None],
                    pool_proj[l].astype(BF16), conv_proj[l].astype(BF16), attn_proj[l].astype(BF16),
                    w_out[l].astype(BF16))
        h = _ffn(h, ffn2_norm[l][None], ffn2_w_gate[l].astype(BF16), ffn2_w_up[l].astype(BF16),
                 ffn2_w_down[l].astype(BF16))

        prompt = lambda t: t[:rows_p].reshape(n_batch, t_pad, -1)
        sample = lambda t: t[rows_p:].reshape(n_streams, dec_seq, -1)
        heads = lambda t: t.reshape(t.shape[0], t.shape[1], N_HEADS, HEAD_DIM)
        outs[0].append(heads(prompt(k)[:, :t_real]))
        outs[1].append(heads(prompt(v)[:, :t_real]))
        outs[2].append(prompt(sq)[:, t_real - POOL_STATE:t_real, :D_POOL])
        outs[3].append(prompt(cx_p)[:, t_real - (CONV_W - 1):t_real])
        outs[4].append(heads(sample(k)))
        outs[5].append(heads(sample(v)))
        outs[6].append(sample(sq)[:, dec_seq - POOL_STATE:, :D_POOL])
        outs[7].append(data(cx_s)[:, dec_seq - (CONV_W - 1):])

    y_prompt = h[:rows_p].reshape(n_batch, t_pad, D_MODEL)[:, N_META:t_real]
    y_sample = h[rows_p:].reshape(n_streams, dec_seq, D_MODEL)
    return (y_prompt, y_sample) + tuple(jnp.stack(o) for o in outs)
```

```python
import functools

import jax
import jax.numpy as jnp
import numpy as np
from jax import lax
from jax.experimental import pallas as pl
from jax.experimental.pallas import tpu as pltpu

F32 = jnp.float32
BF16 = jnp.bfloat16

D_MODEL = 1024
N_META = 16
D_POOL = 256
POOL_WINDOWS = (2, 4, 8, 16)
POOL_GROUP = 64
POOL_STATE = 15
D_CONV = 256
CONV_W = 3
N_HEADS = 8
HEAD_DIM = 64
D_ATTN = N_HEADS * HEAD_DIM
D_FF = 2816
RMS_EPS = 1e-6
LOG2E = 1.4426950408889634
Q_SCALE = HEAD_DIM ** -0.5 * LOG2E
SP_CLAMP = 64.0
N_GATE = 3 * D_MODEL
D_SEQ = D_POOL + 3 * D_CONV

ATT_BLOCK = 256
ATT_HEADS = 4
HALO = 16
TM = 512
PROJ_CHUNK = 512
VMEM_LIMIT = 56 * 1024 * 1024


def _params(sem):
    return pltpu.CompilerParams(dimension_semantics=sem, vmem_limit_bytes=VMEM_LIMIT)


def _rms_rows(x, gain):
    ms = jnp.mean(x * x, axis=-1, keepdims=True)
    return x * lax.rsqrt(ms + RMS_EPS) * gain


def _ffn_kernel(x_ref, g_ref, wg_ref, wu_ref, wd_ref, o_ref):
    x = x_ref[...]
    xn = _rms_rows(x, g_ref[...]).astype(BF16)
    g = jnp.dot(xn, wg_ref[...], preferred_element_type=F32)
    u = jnp.dot(xn, wu_ref[...], preferred_element_type=F32)
    a = (g * jax.nn.sigmoid(g) * u).astype(BF16)
    o_ref[...] = x + 0.5 * jnp.dot(a, wd_ref[...], preferred_element_type=F32)


def _resident(shape):
    return pl.BlockSpec(shape, lambda *_: (0,) * len(shape), pipeline_mode=pl.Buffered(1))


def _ffn(x, gain, wg, wu, wd):
    m = x.shape[0]
    return pl.pallas_call(
        _ffn_kernel,
        grid=(m // TM,),
        in_specs=[
            pl.BlockSpec((TM, D_MODEL), lambda i: (i, 0)),
            pl.BlockSpec((1, D_MODEL), lambda i: (0, 0)),
            _resident((D_MODEL, D_FF)),
            _resident((D_MODEL, D_FF)),
            _resident((D_FF, D_MODEL)),
        ],
        out_specs=pl.BlockSpec((TM, D_MODEL), lambda i: (i, 0)),
        out_shape=jax.ShapeDtypeStruct((m, D_MODEL), F32),
        compiler_params=_params(("parallel",)),
        name="ffn",
    )(x, gain, wg, wu, wd)


def _head_rms(p, seg, gain):
    pp = p * p
    hi = pp.astype(BF16)
    lo = (pp - hi.astype(F32)).astype(BF16)
    ms = jnp.dot(hi, seg, preferred_element_type=F32) + jnp.dot(lo, seg, preferred_element_type=F32)
    return p * lax.rsqrt(ms + RMS_EPS) * gain


def _proj_kernel(h_ref, g_ref, wg_ref, ws_ref, wq_ref, wk_ref, wv_ref, qn_ref, kn_ref, seg_ref,
                 gate_ref, seq_ref, q_ref, k_ref, kb_ref, v_ref, vb_ref):
    u = _rms_rows(h_ref[...], g_ref[...]).astype(BF16)
    ch = PROJ_CHUNK
    for c in range(N_GATE // ch):
        p = jnp.dot(u, wg_ref[:, c * ch:(c + 1) * ch], preferred_element_type=F32)
        gate_ref[:, c * ch:(c + 1) * ch] = jax.nn.sigmoid(p).astype(BF16)
    for c in range(D_SEQ // ch):
        seq_ref[:, c * ch:(c + 1) * ch] = jnp.dot(u, ws_ref[:, c * ch:(c + 1) * ch], preferred_element_type=F32)
    q = _head_rms(jnp.dot(u, wq_ref[...], preferred_element_type=F32), seg_ref[...], qn_ref[...])
    q_ref[...] = (q * Q_SCALE).astype(BF16)
    k = _head_rms(jnp.dot(u, wk_ref[...], preferred_element_type=F32), seg_ref[...], kn_ref[...])
    k_ref[...] = k
    kb_ref[...] = k.astype(BF16)
    v = jnp.dot(u, wv_ref[...], preferred_element_type=F32)
    v_ref[...] = v
    vb_ref[...] = v.astype(BF16)


def _proj(h, gain, wg, ws, wq, wk, wv, qn, kn, seg):
    m = h.shape[0]
    row = lambda i: (i, 0)
    const = lambda i: (0, 0)
    widths = [(N_GATE, BF16), (D_SEQ, F32), (D_ATTN, BF16), (D_ATTN, F32), (D_ATTN, BF16), (D_ATTN, F32), (D_ATTN, BF16)]
    return pl.pallas_call(
        _proj_kernel,
        grid=(m // TM,),
        in_specs=[
            pl.BlockSpec((TM, D_MODEL), row),
            pl.BlockSpec((1, D_MODEL), const),
            _resident((D_MODEL, N_GATE)),
            _resident((D_MODEL, D_SEQ)),
            _resident((D_MODEL, D_ATTN)),
            _resident((D_MODEL, D_ATTN)),
            _resident((D_MODEL, D_ATTN)),
            pl.BlockSpec((1, D_ATTN), const),
            pl.BlockSpec((1, D_ATTN), const),
            _resident((D_ATTN, D_ATTN)),
        ],
        out_specs=[pl.BlockSpec((TM, w), row) for w, _ in widths],
        out_shape=[jax.ShapeDtypeStruct((m, w), dt) for w, dt in widths],
        compiler_params=_params(("parallel",)),
        name="proj",
    )(h, gain, wg, ws, wq, wk, wv, qn, kn, seg)


def _scores(qs, kblk):
    return [lax.dot_general(q, kblk, (((1,), (1,)), ((), ())), preferred_element_type=F32) for q in qs]


def _softplus_bits(zs, valid):
    out = []
    for z in zs:
        sp = jnp.maximum(jnp.log(1.0 + jnp.exp2(jnp.minimum(z, SP_CLAMP))) * LOG2E, z)
        if valid is not None:
            sp = jnp.where(valid, sp, 0.0)
        out.append(sp.astype(BF16))
    return out


def _stick_weights(zs, sps, tri, carries, valid):
    cums = [jnp.dot(sp, tri, preferred_element_type=F32) for sp in sps]
    probs = []
    for z, cum, carry in zip(zs, cums, carries):
        a = jnp.exp2(z - cum - carry)
        if valid is not None:
            a = jnp.where(valid, a, 0.0)
        probs.append(a.astype(BF16))
    return probs, [carry + cum[:, 0:1] for carry, cum in zip(carries, cums)]


def _attn_prompt_kernel(q_ref, k_ref, v_ref, tri_ref, o_ref, acc_ref, car_ref, z_ref, p_ref):
    i = pl.program_id(2)
    blk = ATT_BLOCK
    nh = ATT_HEADS
    lane_head = lax.broadcasted_iota(jnp.int32, (blk, nh * HEAD_DIM), 1) // HEAD_DIM
    q2 = q_ref[...]
    qs = [jnp.where(lane_head == h, q2, jnp.zeros_like(q2)) for h in range(nh)]
    tri = tri_ref[...]

    def scores(j):
        kblk = k_ref[pl.ds(pl.multiple_of(j * blk, blk), blk), :]
        return _scores(qs, kblk)

    def weighted_values(j):
        vblk = v_ref[pl.ds(pl.multiple_of(j * blk, blk), blk), :]
        vstack = jnp.concatenate([jnp.where(lane_head == h, vblk, jnp.zeros_like(vblk)) for h in range(nh)], axis=0)
        return jnp.dot(p_ref[...], vstack, preferred_element_type=F32)

    def weights(zs, sps, valid):
        probs, cars = _stick_weights(zs, sps, tri, [car_ref[h] for h in range(nh)], valid)
        for h in range(nh):
            p_ref[:, h * blk:(h + 1) * blk] = probs[h]
            car_ref[h] = cars[h]

    def store_scores(zs):
        for h in range(nh):
            z_ref[h] = zs[h]

    car_ref[...] = jnp.zeros_like(car_ref)
    store_scores(scores(i))
    z_next = scores(jnp.maximum(i - 1, 0))
    row = lax.broadcasted_iota(jnp.int32, (blk, blk), 0)
    col = lax.broadcasted_iota(jnp.int32, (blk, blk), 1)
    zs = [z_ref[h] for h in range(nh)]
    weights(zs, _softplus_bits(zs, col < row), col < row)
    store_scores(z_next)
    acc_ref[...] = jnp.zeros_like(acc_ref)

    def body(t, c):
        j = i - t
        zs = [z_ref[h] for h in range(nh)]
        sps = _softplus_bits(zs, None)
        acc_ref[...] += weighted_values(j + 1)
        z_next = scores(jnp.maximum(j - 1, 0))
        weights(zs, sps, None)
        store_scores(z_next)
        return c

    lax.fori_loop(1, i + 1, body, 0)
    o_ref[...] = acc_ref[...] + weighted_values(0)


def _attn_prompt(q, kb, vb, tri, n_batch, t_pad):
    m = q.shape[0]
    nq = t_pad // ATT_BLOCK
    pair = ATT_HEADS * HEAD_DIM
    return pl.pallas_call(
        _attn_prompt_kernel,
        grid=(n_batch, N_HEADS // ATT_HEADS, nq),
        in_specs=[
            pl.BlockSpec((ATT_BLOCK, pair), lambda b, h, i: (b * nq + i, h)),
            pl.BlockSpec((t_pad, pair), lambda b, h, i: (b, h)),
            pl.BlockSpec((t_pad, pair), lambda b, h, i: (b, h)),
            pl.BlockSpec((ATT_BLOCK, ATT_BLOCK), lambda b, h, i: (0, 0)),
        ],
        out_specs=pl.BlockSpec((ATT_BLOCK, pair), lambda b, h, i: (b * nq + i, h)),
        out_shape=jax.ShapeDtypeStruct((m, D_ATTN), F32),
        scratch_shapes=[pltpu.VMEM((ATT_BLOCK, pair), F32), pltpu.VMEM((ATT_HEADS, ATT_BLOCK, 1), F32),
                        pltpu.VMEM((ATT_HEADS, ATT_BLOCK, ATT_BLOCK), F32),
                        pltpu.VMEM((ATT_BLOCK, ATT_HEADS * ATT_BLOCK), BF16)],
        compiler_params=_params(("parallel", "parallel", "arbitrary")),
        name="attn_prompt",
    )(q, kb, vb, tri)


def _attn_sample_kernel(q_ref, kn_ref, vn_ref, kc_ref, vc_ref, tri_ref, oin_ref, o_ref, *, dec_seq, past_len):
    del oin_ref
    blk = ATT_BLOCK
    rows = N_HEADS * dec_seq
    q = q_ref[...]
    lane_head = lax.broadcasted_iota(jnp.int32, (dec_seq, D_ATTN), 1) // HEAD_DIM
    qs = jnp.concatenate([jnp.where(lane_head == h, q, jnp.zeros_like(q)) for h in range(N_HEADS)], axis=0)
    tri = tri_ref[...]

    pad = jnp.zeros((blk - dec_seq, D_ATTN), BF16)
    k_new = jnp.concatenate([kn_ref[...], pad], axis=0)
    v_new = jnp.concatenate([vn_ref[...], pad], axis=0)
    tq = lax.broadcasted_iota(jnp.int32, (rows, blk), 0) % dec_seq
    col = lax.broadcasted_iota(jnp.int32, (rows, blk), 1)
    zs = _scores([qs], k_new)
    (a,), (car,) = _stick_weights(zs, _softplus_bits(zs, col < tq), tri, [jnp.zeros((rows, 1), F32)], col < tq)
    acc = jnp.dot(a, v_new, preferred_element_type=F32)
    for j in reversed(range(past_len // blk)):
        kblk = kc_ref[0, 0, pl.ds(j * blk, blk), :]
        vblk = vc_ref[0, 0, pl.ds(j * blk, blk), :]
        zs = _scores([qs], kblk)
        (a,), (car,) = _stick_weights(zs, _softplus_bits(zs, None), tri, [car], None)
        acc = acc + jnp.dot(a, vblk, preferred_element_type=F32)
    o = jnp.zeros((dec_seq, D_ATTN), F32)
    for h in range(N_HEADS):
        o = o + jnp.where(lane_head == h, acc[h * dec_seq:(h + 1) * dec_seq, :], 0.0)
    o_ref[...] = o


def _attn_sample(q, kb, vb, o_all, cache_k, cache_v, layer, tri, row0, n_streams, dec_seq):
    past_len = cache_k.shape[2]
    assert past_len % ATT_BLOCK == 0 and row0 % dec_seq == 0 and dec_seq % 16 == 0
    blk0 = row0 // dec_seq
    kern = functools.partial(_attn_sample_kernel, dec_seq=dec_seq, past_len=past_len)
    return pl.pallas_call(
        kern,
        grid=(n_streams,),
        in_specs=[
            pl.BlockSpec((dec_seq, D_ATTN), lambda b: (blk0 + b, 0)),
            pl.BlockSpec((dec_seq, D_ATTN), lambda b: (blk0 + b, 0)),
            pl.BlockSpec((dec_seq, D_ATTN), lambda b: (blk0 + b, 0)),
            pl.BlockSpec((1, 1, past_len, D_ATTN), lambda b: (layer, b, 0, 0)),
            pl.BlockSpec((1, 1, past_len, D_ATTN), lambda b: (layer, b, 0, 0)),
            pl.BlockSpec((ATT_BLOCK, ATT_BLOCK), lambda b: (0, 0)),
            pl.BlockSpec(memory_space=pl.ANY),
        ],
        out_specs=pl.BlockSpec((dec_seq, D_ATTN), lambda b: (blk0 + b, 0)),
        out_shape=jax.ShapeDtypeStruct(o_all.shape, F32),
        input_output_aliases={6: 0},
        compiler_params=_params(("parallel",)),
        name="attn_sample",
    )(q, kb, vb, cache_k, cache_v, tri, o_all)


def _seqmix_kernel(main_ref, halo_ref, cw_ref, pm_ref, cv_ref, cx_ref, *, pos_offset):
    i = pl.program_id(1)
    tm = main_ref.shape[0]
    main = main_ref[...]
    halo = jnp.where(i == 0, 0.0, halo_ref[...])
    ext = jnp.concatenate([halo, main], axis=0)
    a = ext[:, 0:D_POOL]
    cx = ext[:, D_POOL + 2 * D_CONV:D_POOL + 3 * D_CONV] * ext[:, D_POOL:D_POOL + D_CONV]

    sums = []
    s = a
    for w in POOL_WINDOWS:
        s = s + pltpu.roll(s, w // 2, 0)
        sums.append(s)
    pos = i * tm + pos_offset + lax.broadcasted_iota(jnp.int32, (HALO + tm, 1), 0) - HALO
    lane_group = lax.broadcasted_iota(jnp.int32, (HALO + tm, D_POOL), 1) // POOL_GROUP
    mean = jnp.zeros_like(a)
    for g, w in enumerate(POOL_WINDOWS):
        cnt = jnp.minimum(pos + 1, w).astype(F32)
        mean = jnp.where(lane_group == g, sums[g] / cnt, mean)
    pm_ref[...] = (mean - a)[HALO:]

    cw = cw_ref[...]
    y = pltpu.roll(cx, 2, 0) * cw[0:1] + pltpu.roll(cx, 1, 0) * cw[1:2] + cx * cw[2:3]
    cv_ref[...] = main[:, D_POOL + D_CONV:D_POOL + 2 * D_CONV] * y[HALO:]
    cx_ref[...] = cx[HALO:]


def _seqmix(p, conv_w, n_batch, t_rows, tm, pos_offset):
    nt = t_rows // tm
    per = tm // HALO
    kern = functools.partial(_seqmix_kernel, pos_offset=pos_offset)
    out = jax.ShapeDtypeStruct((n_batch * t_rows, D_POOL), F32)
    return pl.pallas_call(
        kern,
        grid=(n_batch, nt),
        in_specs=[
            pl.BlockSpec((tm, D_SEQ), lambda b, i: (b * nt + i, 0)),
            pl.BlockSpec((HALO, D_SEQ), lambda b, i: (jnp.maximum((b * nt + i) * per - 1, 0), 0)),
            pl.BlockSpec((8, D_CONV), lambda b, i: (0, 0)),
        ],
        out_specs=[pl.BlockSpec((tm, D_POOL), lambda b, i: (b * nt + i, 0))] * 3,
        out_shape=[out] * 3,
        compiler_params=_params(("parallel", "arbitrary")),
        name="seqmix",
    )(p, p, conv_w)


def _mixout_kernel(h_ref, g_ref, pm_ref, cv_ref, o_ref, wgrp_ref, ps_ref, pp_ref, cp_ref, ap_ref, wo_ref, out_ref):
    ya = jnp.dot(pm_ref[...].astype(BF16), wgrp_ref[...], preferred_element_type=F32) * ps_ref[...]
    ya = jnp.dot(ya.astype(BF16), pp_ref[...], preferred_element_type=F32)
    yb = jnp.dot(cv_ref[...].astype(BF16), cp_ref[...], preferred_element_type=F32)
    yc = jnp.dot(o_ref[...].astype(BF16), ap_ref[...], preferred_element_type=F32)
    g = g_ref[...].astype(F32)
    mixed = g[:, 0:D_MODEL] * ya + g[:, D_MODEL:2 * D_MODEL] * yb + g[:, 2 * D_MODEL:3 * D_MODEL] * yc
    out_ref[...] = h_ref[...] + jnp.dot(mixed.astype(BF16), wo_ref[...], preferred_element_type=F32)


def _mixout(h, gates, pm, cv, o, wgrp, pscale, pool_proj, conv_proj, attn_proj, w_out):
    m = h.shape[0]
    row = lambda i: (i, 0)
    const = lambda i: (0, 0)
    return pl.pallas_call(
        _mixout_kernel,
        grid=(m // TM,),
        in_specs=[
            pl.BlockSpec((TM, D_MODEL), row),
            pl.BlockSpec((TM, N_GATE), row),
            pl.BlockSpec((TM, D_POOL), row),
            pl.BlockSpec((TM, D_CONV), row),
            pl.BlockSpec((TM, D_ATTN), row),
            pl.BlockSpec((D_POOL, D_POOL), const),
            pl.BlockSpec((1, D_POOL), const),
            pl.BlockSpec((D_POOL, D_MODEL), const),
            pl.BlockSpec((D_CONV, D_MODEL), const),
            pl.BlockSpec((D_ATTN, D_MODEL), const),
            pl.BlockSpec((D_MODEL, D_MODEL), const),
        ],
        out_specs=pl.BlockSpec((TM, D_MODEL), row),
        out_shape=jax.ShapeDtypeStruct((m, D_MODEL), F32),
        compiler_params=_params(("parallel",)),
        name="mixout",
    )(h, gates, pm, cv, o, wgrp, pscale, pool_proj, conv_proj, attn_proj, w_out)


def _block_diag(w):
    g, c, d = w.shape
    eye = jnp.eye(g, dtype=w.dtype)
    return (eye[:, None, :, None] * w[:, :, None, :]).reshape(g * c, g * d)


def kernel(x_prompt, x_sample, cache_k, cache_v, state_pool, state_conv, meta, ffn1_norm, ffn1_w_gate, ffn1_w_up, ffn1_w_down, mix_norm, w_in, pool_w, pool_scale, pool_proj, conv_w, conv_proj, q_norm, k_norm, attn_proj, w_out, ffn2_norm, ffn2_w_gate, ffn2_w_up, ffn2_w_down):
    n_batch, seq, _ = x_prompt.shape
    n_streams, dec_seq, _ = x_sample.shape
    depth = w_in.shape[0]
    past_len = cache_k.shape[2]
    t_real = N_META + seq
    t_pad = -(-t_real // ATT_BLOCK) * ATT_BLOCK
    rows_p = n_batch * t_pad
    rows_s = n_streams * dec_seq
    m_rows = rows_p + rows_s
    assert m_rows % TM == 0 and dec_seq > POOL_STATE and (HALO + dec_seq) % 8 == 0

    tail = jnp.zeros((t_pad - t_real, D_MODEL), F32)
    pieces = [piece for b in range(n_batch) for piece in (meta.astype(F32), x_prompt[b], tail)]
    h = jnp.concatenate(pieces + [x_sample.reshape(rows_s, D_MODEL)], axis=0)
    cache_kb = cache_k.reshape(depth, n_streams, past_len, D_ATTN).astype(BF16)
    cache_vb = cache_v.reshape(depth, n_streams, past_len, D_ATTN).astype(BF16)

    c_q = D_SEQ
    c_k, c_v, c_g = c_q + D_ATTN, c_q + 2 * D_ATTN, c_q + 3 * D_ATTN
    tri = (np.arange(ATT_BLOCK)[:, None] >= np.arange(ATT_BLOCK)[None, :])
    tri = jnp.asarray(tri, BF16)
    seg = jnp.asarray(np.kron(np.eye(N_HEADS), np.full((HEAD_DIM, HEAD_DIM), 1.0 / HEAD_DIM)), BF16)

    outs = [[] for _ in range(8)]
    for l in range(depth):
        h = _ffn(h, ffn1_norm[l][None], ffn1_w_gate[l].astype(BF16), ffn1_w_up[l].astype(BF16),
                 ffn1_w_down[l].astype(BF16))
        w = w_in[l].astype(BF16)
        gates, sq, q, k, kb, v, vb = _proj(
            h, mix_norm[l][None], w[:, c_g:], w[:, :c_q], w[:, c_q:c_k], w[:, c_k:c_v], w[:, c_v:c_g],
            q_norm[l].reshape(1, D_ATTN), k_norm[l].reshape(1, D_ATTN), seg)

        o = _attn_prompt(q, kb, vb, tri, n_batch, t_pad)
        o = _attn_sample(q, kb, vb, o, cache_kb, cache_vb, l, tri, rows_p, n_streams, dec_seq)

        cw = jnp.zeros((8, D_CONV), F32).at[:CONV_W].set(conv_w[l])
        pm_p, cv_p, cx_p = _seqmix(sq, cw, n_batch, t_pad, ATT_BLOCK, 0)
        seq_s = sq[rows_p:].reshape(n_streams, dec_seq, D_SEQ)
        hist = jnp.zeros((n_streams, HALO, D_SEQ), F32)
        hist = hist.at[:, HALO - POOL_STATE:, 0:D_POOL].set(state_pool[l])
        hist = hist.at[:, HALO - (CONV_W - 1):, D_POOL:D_POOL + D_CONV].set(state_conv[l])
        hist = hist.at[:, :, D_POOL + 2 * D_CONV:].set(1.0)
        ext_s = jnp.concatenate([hist, seq_s], axis=1).reshape(n_streams * (HALO + dec_seq), D_SEQ)
        pm_s, cv_s, cx_s = _seqmix(ext_s, cw, 1, ext_s.shape[0], ext_s.shape[0], past_len)
        data = lambda t: t.reshape(n_streams, HALO + dec_seq, -1)[:, HALO:]
        pm = jnp.concatenate([pm_p, data(pm_s).reshape(rows_s, D_POOL)], axis=0)
        cv = jnp.concatenate([cv_p, data(cv_s).reshape(rows_s, D_CONV)], axis=0)

        h = _mixout(h, gates, pm, cv, o, _block_diag(pool_w[l]).astype(BF16), pool_scale[l][None],
                    pool_proj[l].astype(BF16), conv_proj[l].astype(BF16), attn_proj[l].astype(BF16),
                    w_out[l].astype(BF16))
        h = _ffn(h, ffn2_norm[l][None], ffn2_w_gate[l].astype(BF16), ffn2_w_up[l].astype(BF16),
                 ffn2_w_down[l].astype(BF16))

        prompt = lambda t, lo, hi: jnp.stack([t[b * t_pad + lo:b * t_pad + hi] for b in range(n_batch)])
        sample = lambda t: t[rows_p:].reshape(n_streams, dec_seq, -1)
        heads = lambda t: t.reshape(t.shape[0], t.shape[1], N_HEADS, HEAD_DIM)
        outs[0].append(heads(prompt(k, 0, t_real)))
        outs[1].append(heads(prompt(v, 0, t_real)))
        outs[2].append(prompt(sq, t_real - POOL_STATE, t_real)[:, :, :D_POOL])
        outs[3].append(prompt(cx_p, t_real - (CONV_W - 1), t_real))
        outs[4].append(heads(sample(k)))
        outs[5].append(heads(sample(v)))
        outs[6].append(sample(sq)[:, dec_seq - POOL_STATE:, :D_POOL])
        outs[7].append(data(cx_s)[:, dec_seq - (CONV_W - 1):])

    y_prompt = jnp.stack([h[b * t_pad + N_META:b * t_pad + t_real] for b in range(n_batch)])
    y_sample = h[rows_p:].reshape(n_streams, dec_seq, D_MODEL)
    return (y_prompt, y_sample) + tuple(jnp.stack(o) for o in outs)
```

```python
import functools

import jax
import jax.numpy as jnp
import numpy as np
from jax import lax
from jax.experimental import pallas as pl
from jax.experimental.pallas import tpu as pltpu

F32 = jnp.float32
BF16 = jnp.bfloat16

D_MODEL = 1024
N_META = 16
D_POOL = 256
POOL_WINDOWS = (2, 4, 8, 16)
POOL_GROUP = 64
POOL_STATE = 15
D_CONV = 256
CONV_W = 3
N_HEADS = 8
HEAD_DIM = 64
D_ATTN = N_HEADS * HEAD_DIM
D_FF = 2816
RMS_EPS = 1e-6
LOG2E = 1.4426950408889634
Q_SCALE = HEAD_DIM ** -0.5 * LOG2E
SP_CLAMP = 64.0
N_GATE = 3 * D_MODEL
D_SEQ = D_POOL + 3 * D_CONV

ATT_BLOCK = 256
ATT_HEADS = 4
HALO = 16
TM = 512
PROJ_CHUNK = 512
VMEM_LIMIT = 56 * 1024 * 1024


def _params(sem):
    return pltpu.CompilerParams(dimension_semantics=sem, vmem_limit_bytes=VMEM_LIMIT)


def _rms_rows(x, gain):
    ms = jnp.mean(x * x, axis=-1, keepdims=True)
    return x * lax.rsqrt(ms + RMS_EPS) * gain


def _ffn_kernel(x_ref, g_ref, wg_ref, wu_ref, wd_ref, o_ref):
    x = x_ref[...]
    xn = _rms_rows(x, g_ref[...]).astype(BF16)
    g = jnp.dot(xn, wg_ref[...], preferred_element_type=F32)
    u = jnp.dot(xn, wu_ref[...], preferred_element_type=F32)
    a = (g * jax.nn.sigmoid(g) * u).astype(BF16)
    o_ref[...] = x + 0.5 * jnp.dot(a, wd_ref[...], preferred_element_type=F32)


def _resident(shape):
    return pl.BlockSpec(shape, lambda *_: (0,) * len(shape), pipeline_mode=pl.Buffered(1))


def _ffn(x, gain, wg, wu, wd):
    m = x.shape[0]
    return pl.pallas_call(
        _ffn_kernel,
        grid=(m // TM,),
        in_specs=[
            pl.BlockSpec((TM, D_MODEL), lambda i: (i, 0)),
            pl.BlockSpec((1, D_MODEL), lambda i: (0, 0)),
            _resident((D_MODEL, D_FF)),
            _resident((D_MODEL, D_FF)),
            _resident((D_FF, D_MODEL)),
        ],
        out_specs=pl.BlockSpec((TM, D_MODEL), lambda i: (i, 0)),
        out_shape=jax.ShapeDtypeStruct((m, D_MODEL), F32),
        compiler_params=_params(("parallel",)),
        name="ffn",
    )(x, gain, wg, wu, wd)


def _pool_conv(ext, pos, cw):
    a = ext[:, 0:D_POOL]
    cx = ext[:, D_POOL + 2 * D_CONV:D_POOL + 3 * D_CONV] * ext[:, D_POOL:D_POOL + D_CONV]
    sums = []
    s = a
    for w in POOL_WINDOWS:
        s = s + pltpu.roll(s, w // 2, 0)
        sums.append(s)
    lane_group = lax.broadcasted_iota(jnp.int32, a.shape, 1) // POOL_GROUP
    mean = jnp.zeros_like(a)
    for g, w in enumerate(POOL_WINDOWS):
        cnt = jnp.clip(pos + 1, 1, w).astype(F32)
        mean = jnp.where(lane_group == g, sums[g] / cnt, mean)
    y = pltpu.roll(cx, 2, 0) * cw[0:1] + pltpu.roll(cx, 1, 0) * cw[1:2] + cx * cw[2:3]
    gated = ext[:, D_POOL + D_CONV:D_POOL + 2 * D_CONV] * y
    return (mean - a)[HALO:], gated[HALO:], cx[HALO:]


def _head_rms(p, seg, gain):
    pp = p * p
    hi = pp.astype(BF16)
    lo = (pp - hi.astype(F32)).astype(BF16)
    ms = jnp.dot(hi, seg, preferred_element_type=F32) + jnp.dot(lo, seg, preferred_element_type=F32)
    return p * lax.rsqrt(ms + RMS_EPS) * gain


def _proj_kernel(h_ref, g_ref, wg_ref, ws_ref, wq_ref, wk_ref, wv_ref, qn_ref, kn_ref, seg_ref, cw_ref,
                 gate_ref, seq_ref, q_ref, k_ref, kb_ref, v_ref, vb_ref, pm_ref, cv_ref, cx_ref, halo_ref,
                 *, t_pad, t_real):
    i = pl.program_id(0)

    @pl.when(i == 0)
    def _():
        halo_ref[...] = jnp.zeros_like(halo_ref)

    u = _rms_rows(h_ref[...], g_ref[...]).astype(BF16)
    ch = PROJ_CHUNK
    for c in range(N_GATE // ch):
        p = jnp.dot(u, wg_ref[:, c * ch:(c + 1) * ch], preferred_element_type=F32)
        gate_ref[:, c * ch:(c + 1) * ch] = jax.nn.sigmoid(p).astype(BF16)
    for c in range(D_SEQ // ch):
        seq_ref[:, c * ch:(c + 1) * ch] = jnp.dot(u, ws_ref[:, c * ch:(c + 1) * ch], preferred_element_type=F32)
    q = _head_rms(jnp.dot(u, wq_ref[...], preferred_element_type=F32), seg_ref[...], qn_ref[...])
    q_ref[...] = (q * Q_SCALE).astype(BF16)
    k = _head_rms(jnp.dot(u, wk_ref[...], preferred_element_type=F32), seg_ref[...], kn_ref[...])
    k_ref[...] = k
    kb_ref[...] = k.astype(BF16)
    v = jnp.dot(u, wv_ref[...], preferred_element_type=F32)
    v_ref[...] = v
    vb_ref[...] = v.astype(BF16)

    tm = seq_ref.shape[0]
    row0 = i * tm
    pos0 = row0 - (row0 // t_pad) * t_pad
    pos = pos0 + lax.broadcasted_iota(jnp.int32, (tm, 1), 0)
    pos = jnp.where(pos >= t_pad, pos - t_pad, pos)
    seq = jnp.where(pos < t_real, seq_ref[...], 0.0)
    ext = jnp.concatenate([halo_ref[...], seq], axis=0)
    pos_ext = jnp.concatenate([jnp.zeros((HALO, 1), jnp.int32), pos], axis=0)
    pm, cv, cx = _pool_conv(ext, pos_ext, cw_ref[...])
    pm_ref[...] = pm.astype(BF16)
    cv_ref[...] = cv.astype(BF16)
    cx_ref[...] = cx
    halo_ref[...] = seq[tm - HALO:]


def _proj(h, gain, wg, ws, wq, wk, wv, qn, kn, seg, conv_w, t_pad, t_real):
    m = h.shape[0]
    assert TM <= t_pad and t_pad - t_real >= HALO
    row = lambda i: (i, 0)
    const = lambda i: (0, 0)
    widths = [(N_GATE, BF16), (D_SEQ, F32), (D_ATTN, BF16), (D_ATTN, F32), (D_ATTN, BF16), (D_ATTN, F32), (D_ATTN, BF16),
              (D_POOL, BF16), (D_CONV, BF16), (D_CONV, F32)]
    return pl.pallas_call(
        functools.partial(_proj_kernel, t_pad=t_pad, t_real=t_real),
        grid=(m // TM,),
        in_specs=[
            pl.BlockSpec((TM, D_MODEL), row),
            pl.BlockSpec((1, D_MODEL), const),
            _resident((D_MODEL, N_GATE)),
            _resident((D_MODEL, D_SEQ)),
            _resident((D_MODEL, D_ATTN)),
            _resident((D_MODEL, D_ATTN)),
            _resident((D_MODEL, D_ATTN)),
            pl.BlockSpec((1, D_ATTN), const),
            pl.BlockSpec((1, D_ATTN), const),
            _resident((D_ATTN, D_ATTN)),
            pl.BlockSpec((8, D_CONV), const),
        ],
        out_specs=[pl.BlockSpec((TM, w), row) for w, _ in widths],
        out_shape=[jax.ShapeDtypeStruct((m, w), dt) for w, dt in widths],
        scratch_shapes=[pltpu.VMEM((HALO, D_SEQ), F32)],
        compiler_params=_params(("arbitrary",)),
        name="proj",
    )(h, gain, wg, ws, wq, wk, wv, qn, kn, seg, conv_w)


def _scores(qs, kblk):
    rows = qs[0].shape[0]
    z = lax.dot_general(jnp.concatenate(qs, axis=0), kblk, (((1,), (1,)), ((), ())), preferred_element_type=F32)
    return [z[h * rows:(h + 1) * rows] for h in range(len(qs))]


def _softplus_bits(zs, valid):
    out = []
    for z in zs:
        sp = jnp.maximum(jnp.log(1.0 + jnp.exp2(jnp.minimum(z, SP_CLAMP))) * LOG2E, z)
        if valid is not None:
            sp = jnp.where(valid, sp, 0.0)
        out.append(sp.astype(BF16))
    return out


def _stick_weights(zs, sps, tri, carries, valid):
    cums = [jnp.dot(sp, tri, preferred_element_type=F32) for sp in sps]
    probs = []
    for z, cum, carry in zip(zs, cums, carries):
        a = jnp.exp2(z - cum - carry)
        if valid is not None:
            a = jnp.where(valid, a, 0.0)
        probs.append(a.astype(BF16))
    return probs, [carry + cum[:, 0:1] for carry, cum in zip(carries, cums)]


def _attn_prompt_kernel(q_ref, k_ref, v_ref, tri_ref, o_ref, acc_ref, car_ref, z_ref, p_ref):
    i = pl.program_id(2)
    blk = ATT_BLOCK
    nh = ATT_HEADS
    lane_head = lax.broadcasted_iota(jnp.int32, (blk, nh * HEAD_DIM), 1) // HEAD_DIM
    q2 = q_ref[...]
    qs = [jnp.where(lane_head == h, q2, jnp.zeros_like(q2)) for h in range(nh)]
    tri = tri_ref[...]

    def scores(j):
        kblk = k_ref[pl.ds(pl.multiple_of(j * blk, blk), blk), :]
        return _scores(qs, kblk)

    def weighted_values(j):
        vblk = v_ref[pl.ds(pl.multiple_of(j * blk, blk), blk), :]
        vstack = jnp.concatenate([jnp.where(lane_head == h, vblk, jnp.zeros_like(vblk)) for h in range(nh)], axis=0)
        return jnp.dot(p_ref[...], vstack, preferred_element_type=F32)

    def weights(zs, sps, valid):
        probs, cars = _stick_weights(zs, sps, tri, [car_ref[h] for h in range(nh)], valid)
        for h in range(nh):
            p_ref[:, h * blk:(h + 1) * blk] = probs[h]
            car_ref[h] = cars[h]

    def store_scores(zs):
        for h in range(nh):
            z_ref[h] = zs[h]

    car_ref[...] = jnp.zeros_like(car_ref)
    store_scores(scores(i))
    z_next = scores(jnp.maximum(i - 1, 0))
    row = lax.broadcasted_iota(jnp.int32, (blk, blk), 0)
    col = lax.broadcasted_iota(jnp.int32, (blk, blk), 1)
    zs = [z_ref[h] for h in range(nh)]
    weights(zs, _softplus_bits(zs, col < row), col < row)
    store_scores(z_next)
    acc_ref[...] = jnp.zeros_like(acc_ref)

    def body(t, c):
        j = i - t
        zs = [z_ref[h] for h in range(nh)]
        sps = _softplus_bits(zs, None)
        acc_ref[...] += weighted_values(j + 1)
        z_next = scores(jnp.maximum(j - 1, 0))
        weights(zs, sps, None)
        store_scores(z_next)
        return c

    lax.fori_loop(1, i + 1, body, 0)
    o_ref[...] = (acc_ref[...] + weighted_values(0)).astype(o_ref.dtype)


def _attn_prompt(q, kb, vb, tri, n_batch, t_pad):
    m = q.shape[0]
    nq = t_pad // ATT_BLOCK
    pair = ATT_HEADS * HEAD_DIM
    return pl.pallas_call(
        _attn_prompt_kernel,
        grid=(n_batch, N_HEADS // ATT_HEADS, nq),
        in_specs=[
            pl.BlockSpec((ATT_BLOCK, pair), lambda b, h, i: (b * nq + i, h)),
            pl.BlockSpec((t_pad, pair), lambda b, h, i: (b, h)),
            pl.BlockSpec((t_pad, pair), lambda b, h, i: (b, h)),
            pl.BlockSpec((ATT_BLOCK, ATT_BLOCK), lambda b, h, i: (0, 0)),
        ],
        out_specs=pl.BlockSpec((ATT_BLOCK, pair), lambda b, h, i: (b * nq + i, h)),
        out_shape=jax.ShapeDtypeStruct((m, D_ATTN), BF16),
        scratch_shapes=[pltpu.VMEM((ATT_BLOCK, pair), F32), pltpu.VMEM((ATT_HEADS, ATT_BLOCK, 1), F32),
                        pltpu.VMEM((ATT_HEADS, ATT_BLOCK, ATT_BLOCK), F32),
                        pltpu.VMEM((ATT_BLOCK, ATT_HEADS * ATT_BLOCK), BF16)],
        compiler_params=_params(("parallel", "parallel", "arbitrary")),
        name="attn_prompt",
    )(q, kb, vb, tri)


def _attn_sample_kernel(q_ref, kn_ref, vn_ref, kc_ref, vc_ref, tri_ref, oin_ref, o_ref, *, dec_seq, past_len):
    del oin_ref
    blk = ATT_BLOCK
    rows = N_HEADS * dec_seq
    q = q_ref[...]
    lane_head = lax.broadcasted_iota(jnp.int32, (dec_seq, D_ATTN), 1) // HEAD_DIM
    qs = jnp.concatenate([jnp.where(lane_head == h, q, jnp.zeros_like(q)) for h in range(N_HEADS)], axis=0)
    tri = tri_ref[...]

    pad = jnp.zeros((blk - dec_seq, D_ATTN), BF16)
    k_new = jnp.concatenate([kn_ref[...], pad], axis=0)
    v_new = jnp.concatenate([vn_ref[...], pad], axis=0)
    tq = lax.broadcasted_iota(jnp.int32, (rows, blk), 0) % dec_seq
    col = lax.broadcasted_iota(jnp.int32, (rows, blk), 1)
    zs = _scores([qs], k_new)
    (a,), (car,) = _stick_weights(zs, _softplus_bits(zs, col < tq), tri, [jnp.zeros((rows, 1), F32)], col < tq)
    acc = jnp.dot(a, v_new, preferred_element_type=F32)
    for j in reversed(range(past_len // blk)):
        kblk = kc_ref[0, 0, pl.ds(j * blk, blk), :].astype(BF16)
        vblk = vc_ref[0, 0, pl.ds(j * blk, blk), :].astype(BF16)
        zs = _scores([qs], kblk)
        (a,), (car,) = _stick_weights(zs, _softplus_bits(zs, None), tri, [car], None)
        acc = acc + jnp.dot(a, vblk, preferred_element_type=F32)
    o = jnp.zeros((dec_seq, D_ATTN), F32)
    for h in range(N_HEADS):
        o = o + jnp.where(lane_head == h, acc[h * dec_seq:(h + 1) * dec_seq, :], 0.0)
    o_ref[...] = o.astype(o_ref.dtype)


def _attn_sample(q, kb, vb, o_all, cache_k, cache_v, layer, tri, row0, n_streams, dec_seq):
    past_len = cache_k.shape[2]
    assert past_len % ATT_BLOCK == 0 and row0 % dec_seq == 0 and dec_seq % 16 == 0
    blk0 = row0 // dec_seq
    kern = functools.partial(_attn_sample_kernel, dec_seq=dec_seq, past_len=past_len)
    return pl.pallas_call(
        kern,
        grid=(n_streams,),
        in_specs=[
            pl.BlockSpec((dec_seq, D_ATTN), lambda b: (blk0 + b, 0)),
            pl.BlockSpec((dec_seq, D_ATTN), lambda b: (blk0 + b, 0)),
            pl.BlockSpec((dec_seq, D_ATTN), lambda b: (blk0 + b, 0)),
            pl.BlockSpec((1, 1, past_len, D_ATTN), lambda b: (layer, b, 0, 0)),
            pl.BlockSpec((1, 1, past_len, D_ATTN), lambda b: (layer, b, 0, 0)),
            pl.BlockSpec((ATT_BLOCK, ATT_BLOCK), lambda b: (0, 0)),
            pl.BlockSpec(memory_space=pl.ANY),
        ],
        out_specs=pl.BlockSpec((dec_seq, D_ATTN), lambda b: (blk0 + b, 0)),
        out_shape=jax.ShapeDtypeStruct(o_all.shape, o_all.dtype),
        input_output_aliases={6: 0},
        compiler_params=_params(("parallel",)),
        name="attn_sample",
    )(q, kb, vb, cache_k, cache_v, tri, o_all)


def _seqmix_kernel(ext_ref, cw_ref, pm_ref, cv_ref, cx_ref, *, pos0):
    ext = ext_ref[...]
    ext = jnp.concatenate([jnp.zeros((HALO, D_SEQ), F32), ext], axis=0)
    pos = jnp.full((ext.shape[0], 1), pos0, jnp.int32)
    pm, cv, cx = _pool_conv(ext, pos, cw_ref[...])
    pm_ref[...] = pm.astype(BF16)
    cv_ref[...] = cv.astype(BF16)
    cx_ref[...] = cx


def _seqmix_sample(ext, conv_w, pos0):
    rows = ext.shape[0]
    full = lambda w: pl.BlockSpec((rows, w), lambda: (0, 0))
    out = lambda dt: jax.ShapeDtypeStruct((rows, D_POOL), dt)
    return pl.pallas_call(
        functools.partial(_seqmix_kernel, pos0=pos0),
        in_specs=[full(D_SEQ), pl.BlockSpec((8, D_CONV), lambda: (0, 0))],
        out_specs=[full(D_POOL)] * 3,
        out_shape=[out(BF16), out(BF16), out(F32)],
        compiler_params=pltpu.CompilerParams(vmem_limit_bytes=VMEM_LIMIT),
        name="seqmix",
    )(ext, conv_w)


def _mixout_kernel(h_ref, g_ref, pm_ref, cv_ref, o_ref, wgrp_ref, ps_ref, pp_ref, cp_ref, ap_ref, wo_ref, out_ref):
    ya = jnp.dot(pm_ref[...], wgrp_ref[...], preferred_element_type=F32) * ps_ref[...]
    ya = jnp.dot(ya.astype(BF16), pp_ref[...], preferred_element_type=F32)
    yb = jnp.dot(cv_ref[...], cp_ref[...], preferred_element_type=F32)
    yc = jnp.dot(o_ref[...], ap_ref[...], preferred_element_type=F32)
    g = g_ref[...].astype(F32)
    mixed = g[:, 0:D_MODEL] * ya + g[:, D_MODEL:2 * D_MODEL] * yb + g[:, 2 * D_MODEL:3 * D_MODEL] * yc
    out_ref[...] = h_ref[...] + jnp.dot(mixed.astype(BF16), wo_ref[...], preferred_element_type=F32)


def _mixout(h, gates, pm, cv, o, wgrp, pscale, pool_proj, conv_proj, attn_proj, w_out):
    m = h.shape[0]
    row = lambda i: (i, 0)
    const = lambda i: (0, 0)
    return pl.pallas_call(
        _mixout_kernel,
        grid=(m // TM,),
        in_specs=[
            pl.BlockSpec((TM, D_MODEL), row),
            pl.BlockSpec((TM, N_GATE), row),
            pl.BlockSpec((TM, D_POOL), row),
            pl.BlockSpec((TM, D_CONV), row),
            pl.BlockSpec((TM, D_ATTN), row),
            pl.BlockSpec((D_POOL, D_POOL), const),
            pl.BlockSpec((1, D_POOL), const),
            pl.BlockSpec((D_POOL, D_MODEL), const),
            pl.BlockSpec((D_CONV, D_MODEL), const),
            pl.BlockSpec((D_ATTN, D_MODEL), const),
            pl.BlockSpec((D_MODEL, D_MODEL), const),
        ],
        out_specs=pl.BlockSpec((TM, D_MODEL), row),
        out_shape=jax.ShapeDtypeStruct((m, D_MODEL), F32),
        compiler_params=_params(("parallel",)),
        name="mixout",
    )(h, gates, pm, cv, o, wgrp, pscale, pool_proj, conv_proj, attn_proj, w_out)


def _block_diag(w):
    g, c, d = w.shape
    eye = jnp.eye(g, dtype=w.dtype)
    return (eye[:, None, :, None] * w[:, :, None, :]).reshape(g * c, g * d)


def kernel(x_prompt, x_sample, cache_k, cache_v, state_pool, state_conv, meta, ffn1_norm, ffn1_w_gate, ffn1_w_up, ffn1_w_down, mix_norm, w_in, pool_w, pool_scale, pool_proj, conv_w, conv_proj, q_norm, k_norm, attn_proj, w_out, ffn2_norm, ffn2_w_gate, ffn2_w_up, ffn2_w_down):
    n_batch, seq, _ = x_prompt.shape
    n_streams, dec_seq, _ = x_sample.shape
    depth = w_in.shape[0]
    past_len = cache_k.shape[2]
    t_real = N_META + seq
    t_pad = -(-t_real // ATT_BLOCK) * ATT_BLOCK
    rows_p = n_batch * t_pad
    rows_s = n_streams * dec_seq
    m_rows = rows_p + rows_s
    assert m_rows % TM == 0 and dec_seq > POOL_STATE and (HALO + dec_seq) % 8 == 0

    tail = jnp.zeros((t_pad - t_real, D_MODEL), F32)
    pieces = [piece for b in range(n_batch) for piece in (meta.astype(F32), x_prompt[b], tail)]
    h = jnp.concatenate(pieces + [x_sample.reshape(rows_s, D_MODEL)], axis=0)
    cache_k2 = cache_k.reshape(depth, n_streams, past_len, D_ATTN)
    cache_v2 = cache_v.reshape(depth, n_streams, past_len, D_ATTN)

    c_q = D_SEQ
    c_k, c_v, c_g = c_q + D_ATTN, c_q + 2 * D_ATTN, c_q + 3 * D_ATTN
    tri = (np.arange(ATT_BLOCK)[:, None] >= np.arange(ATT_BLOCK)[None, :])
    tri = jnp.asarray(tri, BF16)
    seg = jnp.asarray(np.kron(np.eye(N_HEADS), np.full((HEAD_DIM, HEAD_DIM), 1.0 / HEAD_DIM)), BF16)

    outs = [[] for _ in range(8)]
    for l in range(depth):
        h = _ffn(h, ffn1_norm[l][None], ffn1_w_gate[l].astype(BF16), ffn1_w_up[l].astype(BF16),
                 ffn1_w_down[l].astype(BF16))
        w = w_in[l].astype(BF16)
        cw = jnp.zeros((8, D_CONV), F32).at[:CONV_W].set(conv_w[l])
        gates, sq, q, k, kb, v, vb, pm, cv, cx_p = _proj(
            h, mix_norm[l][None], w[:, c_g:], w[:, :c_q], w[:, c_q:c_k], w[:, c_k:c_v], w[:, c_v:c_g],
            q_norm[l].reshape(1, D_ATTN), k_norm[l].reshape(1, D_ATTN), seg, cw, t_pad, t_real)

        o = _attn_prompt(q, kb, vb, tri, n_batch, t_pad)
        o = _attn_sample(q, kb, vb, o, cache_k2, cache_v2, l, tri, rows_p, n_streams, dec_seq)

        seq_s = sq[rows_p:].reshape(n_streams, dec_seq, D_SEQ)
        hist = jnp.zeros((n_streams, HALO, D_SEQ), F32)
        hist = hist.at[:, HALO - POOL_STATE:, 0:D_POOL].set(state_pool[l])
        hist = hist.at[:, HALO - (CONV_W - 1):, D_POOL:D_POOL + D_CONV].set(state_conv[l])
        hist = hist.at[:, :, D_POOL + 2 * D_CONV:].set(1.0)
        ext_s = jnp.concatenate([hist, seq_s], axis=1).reshape(n_streams * (HALO + dec_seq), D_SEQ)
        pm_s, cv_s, cx_s = _seqmix_sample(ext_s, cw, past_len)
        data = lambda t: t.reshape(n_streams, HALO + dec_seq, -1)[:, HALO:]
        pm = lax.dynamic_update_slice(pm, data(pm_s).reshape(rows_s, D_POOL), (rows_p, 0))
        cv = lax.dynamic_update_slice(cv, data(cv_s).reshape(rows_s, D_CONV), (rows_p, 0))

        h = _mixout(h, gates, pm, cv, o, _block_diag(pool_w[l]).astype(BF16), pool_scale[l][None],
                    pool_proj[l].astype(BF16), conv_proj[l].astype(BF16), attn_proj[l].astype(BF16),
                    w_out[l].astype(BF16))
        h = _ffn(h, ffn2_norm[l][None], ffn2_w_gate[l].astype(BF16), ffn2_w_up[l].astype(BF16),
                 ffn2_w_down[l].astype(BF16))

        prompt = lambda t, lo, hi: jnp.stack([t[b * t_pad + lo:b * t_pad + hi] for b in range(n_batch)])
        sample = lambda t: t[rows_p:].reshape(n_streams, dec_seq, -1)
        heads = lambda t: t.reshape(t.shape[0], t.shape[1], N_HEADS, HEAD_DIM)
        outs[0].append(heads(prompt(k, 0, t_real)))
        outs[1].append(heads(prompt(v, 0, t_real)))
        outs[2].append(prompt(sq, t_real - POOL_STATE, t_real)[:, :, :D_POOL])
        outs[3].append(prompt(cx_p, t_real - (CONV_W - 1), t_real))
        outs[4].append(heads(sample(k)))
        outs[5].append(heads(sample(v)))
        outs[6].append(sample(sq)[:, dec_seq - POOL_STATE:, :D_POOL])
        outs[7].append(data(cx_s)[:, dec_seq - (CONV_W - 1):])

    y_prompt = jnp.stack([h[b * t_pad + N_META:b * t_pad + t_real] for b in range(n_batch)])
    y_sample = h[rows_p:].reshape(n_streams, dec_seq, D_MODEL)
    return (y_prompt, y_sample) + tuple(jnp.stack(o) for o in outs)
```

```python
import functools

import jax
import jax.numpy as jnp
import numpy as np
from jax import lax
from jax.experimental import pallas as pl
from jax.experimental.pallas import tpu as pltpu

F32 = jnp.float32
BF16 = jnp.bfloat16

D_MODEL = 1024
N_META = 16
D_POOL = 256
POOL_WINDOWS = (2, 4, 8, 16)
POOL_GROUP = 64
POOL_STATE = 15
D_CONV = 256
CONV_W = 3
N_HEADS = 8
HEAD_DIM = 64
D_ATTN = N_HEADS * HEAD_DIM
D_FF = 2816
RMS_EPS = 1e-6
LOG2E = 1.4426950408889634
Q_SCALE = HEAD_DIM ** -0.5 * LOG2E
SP_CLAMP = 64.0
CARRY_SETTLED = 160.0
N_GATE = 3 * D_MODEL
D_SEQ = D_POOL + 3 * D_CONV

ATT_BLOCK = 256
ATT_HEADS = 4
HALO = 16
TM = 512
PROJ_CHUNK = 512
VMEM_LIMIT = 56 * 1024 * 1024


def _params(sem):
    return pltpu.CompilerParams(dimension_semantics=sem, vmem_limit_bytes=VMEM_LIMIT)


def _rms_rows(x, gain):
    ms = jnp.mean(x * x, axis=-1, keepdims=True)
    return x * lax.rsqrt(ms + RMS_EPS) * gain


def _ffn_kernel(x_ref, g_ref, wg_ref, wu_ref, wd_ref, o_ref):
    x = x_ref[...]
    xn = _rms_rows(x, g_ref[...]).astype(BF16)
    g = jnp.dot(xn, wg_ref[...], preferred_element_type=F32)
    u = jnp.dot(xn, wu_ref[...], preferred_element_type=F32)
    a = (g * jax.nn.sigmoid(g) * u).astype(BF16)
    o_ref[...] = x + 0.5 * jnp.dot(a, wd_ref[...], preferred_element_type=F32)


def _resident(shape):
    return pl.BlockSpec(shape, lambda *_: (0,) * len(shape), pipeline_mode=pl.Buffered(1))


def _ffn(x, gain, wg, wu, wd):
    m = x.shape[0]
    return pl.pallas_call(
        _ffn_kernel,
        grid=(m // TM,),
        in_specs=[
            pl.BlockSpec((TM, D_MODEL), lambda i: (i, 0)),
            pl.BlockSpec((1, D_MODEL), lambda i: (0, 0)),
            _resident((D_MODEL, D_FF)),
            _resident((D_MODEL, D_FF)),
            _resident((D_FF, D_MODEL)),
        ],
        out_specs=pl.BlockSpec((TM, D_MODEL), lambda i: (i, 0)),
        out_shape=jax.ShapeDtypeStruct((m, D_MODEL), F32),
        compiler_params=_params(("parallel",)),
        name="ffn",
    )(x, gain, wg, wu, wd)


def _pool_conv(ext, pos, cw):
    a = ext[:, 0:D_POOL]
    cx = ext[:, D_POOL + 2 * D_CONV:D_POOL + 3 * D_CONV] * ext[:, D_POOL:D_POOL + D_CONV]
    sums = []
    s = a
    for w in POOL_WINDOWS:
        s = s + pltpu.roll(s, w // 2, 0)
        sums.append(s)
    lane_group = lax.broadcasted_iota(jnp.int32, a.shape, 1) // POOL_GROUP
    mean = jnp.zeros_like(a)
    for g, w in enumerate(POOL_WINDOWS):
        cnt = jnp.clip(pos + 1, 1, w).astype(F32)
        mean = jnp.where(lane_group == g, sums[g] / cnt, mean)
    y = pltpu.roll(cx, 2, 0) * cw[0:1] + pltpu.roll(cx, 1, 0) * cw[1:2] + cx * cw[2:3]
    gated = ext[:, D_POOL + D_CONV:D_POOL + 2 * D_CONV] * y
    return (mean - a)[HALO:], gated[HALO:], cx[HALO:]


def _head_rms(p, seg, gain):
    pp = p * p
    hi = pp.astype(BF16)
    lo = (pp - hi.astype(F32)).astype(BF16)
    ms = jnp.dot(hi, seg, preferred_element_type=F32) + jnp.dot(lo, seg, preferred_element_type=F32)
    return p * lax.rsqrt(ms + RMS_EPS) * gain


def _proj_kernel(h_ref, g_ref, wg_ref, ws_ref, wq_ref, wk_ref, wv_ref, qn_ref, kn_ref, seg_ref, cw_ref,
                 gate_ref, seq_ref, q_ref, k_ref, kb_ref, v_ref, vb_ref, pm_ref, cv_ref, cx_ref, halo_ref,
                 *, t_pad, t_real):
    i = pl.program_id(0)

    @pl.when(i == 0)
    def _():
        halo_ref[...] = jnp.zeros_like(halo_ref)

    u = _rms_rows(h_ref[...], g_ref[...]).astype(BF16)
    ch = PROJ_CHUNK
    for c in range(N_GATE // ch):
        p = jnp.dot(u, wg_ref[:, c * ch:(c + 1) * ch], preferred_element_type=F32)
        gate_ref[:, c * ch:(c + 1) * ch] = jax.nn.sigmoid(p).astype(BF16)
    for c in range(D_SEQ // ch):
        seq_ref[:, c * ch:(c + 1) * ch] = jnp.dot(u, ws_ref[:, c * ch:(c + 1) * ch], preferred_element_type=F32)
    q = _head_rms(jnp.dot(u, wq_ref[...], preferred_element_type=F32), seg_ref[...], qn_ref[...])
    q_ref[...] = (q * Q_SCALE).astype(BF16)
    k = _head_rms(jnp.dot(u, wk_ref[...], preferred_element_type=F32), seg_ref[...], kn_ref[...])
    k_ref[...] = k
    kb_ref[...] = k.astype(BF16)
    v = jnp.dot(u, wv_ref[...], preferred_element_type=F32)
    v_ref[...] = v
    vb_ref[...] = v.astype(BF16)

    tm = seq_ref.shape[0]
    row0 = i * tm
    pos0 = row0 - (row0 // t_pad) * t_pad
    pos = pos0 + lax.broadcasted_iota(jnp.int32, (tm, 1), 0)
    pos = jnp.where(pos >= t_pad, pos - t_pad, pos)
    seq = jnp.where(pos < t_real, seq_ref[...], 0.0)
    ext = jnp.concatenate([halo_ref[...], seq], axis=0)
    pos_ext = jnp.concatenate([jnp.zeros((HALO, 1), jnp.int32), pos], axis=0)
    pm, cv, cx = _pool_conv(ext, pos_ext, cw_ref[...])
    pm_ref[...] = pm.astype(BF16)
    cv_ref[...] = cv.astype(BF16)
    cx_ref[...] = cx
    halo_ref[...] = seq[tm - HALO:]


def _proj(h, gain, wg, ws, wq, wk, wv, qn, kn, seg, conv_w, t_pad, t_real):
    m = h.shape[0]
    assert TM <= t_pad and t_pad - t_real >= HALO
    row = lambda i: (i, 0)
    const = lambda i: (0, 0)
    widths = [(N_GATE, BF16), (D_SEQ, F32), (D_ATTN, BF16), (D_ATTN, F32), (D_ATTN, BF16), (D_ATTN, F32), (D_ATTN, BF16),
              (D_POOL, BF16), (D_CONV, BF16), (D_CONV, F32)]
    return pl.pallas_call(
        functools.partial(_proj_kernel, t_pad=t_pad, t_real=t_real),
        grid=(m // TM,),
        in_specs=[
            pl.BlockSpec((TM, D_MODEL), row),
            pl.BlockSpec((1, D_MODEL), const),
            _resident((D_MODEL, N_GATE)),
            _resident((D_MODEL, D_SEQ)),
            _resident((D_MODEL, D_ATTN)),
            _resident((D_MODEL, D_ATTN)),
            _resident((D_MODEL, D_ATTN)),
            pl.BlockSpec((1, D_ATTN), const),
            pl.BlockSpec((1, D_ATTN), const),
            _resident((D_ATTN, D_ATTN)),
            pl.BlockSpec((8, D_CONV), const),
        ],
        out_specs=[pl.BlockSpec((TM, w), row) for w, _ in widths],
        out_shape=[jax.ShapeDtypeStruct((m, w), dt) for w, dt in widths],
        scratch_shapes=[pltpu.VMEM((HALO, D_SEQ), F32)],
        compiler_params=_params(("arbitrary",)),
        name="proj",
    )(h, gain, wg, ws, wq, wk, wv, qn, kn, seg, conv_w)


def _scores(qs, kblk):
    rows = qs[0].shape[0]
    z = lax.dot_general(jnp.concatenate(qs, axis=0), kblk, (((1,), (1,)), ((), ())), preferred_element_type=F32)
    return [z[h * rows:(h + 1) * rows] for h in range(len(qs))]


def _softplus_bits(zs, valid):
    out = []
    for z in zs:
        sp = jnp.maximum(jnp.log(1.0 + jnp.exp2(jnp.minimum(z, SP_CLAMP))) * LOG2E, z)
        if valid is not None:
            sp = jnp.where(valid, sp, 0.0)
        out.append(sp.astype(BF16))
    return out


def _stick_weights(zs, sps, tri, carries, valid):
    cums = [jnp.dot(sp, tri, preferred_element_type=F32) for sp in sps]
    probs = []
    for z, cum, carry in zip(zs, cums, carries):
        a = jnp.exp2(z - cum - carry)
        if valid is not None:
            a = jnp.where(valid, a, 0.0)
        probs.append(a.astype(BF16))
    return probs, [carry + cum[:, 0:1] for carry, cum in zip(carries, cums)]


def _attn_prompt_kernel(q_ref, k_ref, v_ref, tri_ref, o_ref, acc_ref, car_ref, z_ref, p_ref):
    i = pl.program_id(2)
    blk = ATT_BLOCK
    nh = ATT_HEADS
    lane_head = lax.broadcasted_iota(jnp.int32, (blk, nh * HEAD_DIM), 1) // HEAD_DIM
    q2 = q_ref[...]
    qs = [jnp.where(lane_head == h, q2, jnp.zeros_like(q2)) for h in range(nh)]
    tri = tri_ref[...]

    def scores(j):
        kblk = k_ref[pl.ds(pl.multiple_of(j * blk, blk), blk), :]
        return _scores(qs, kblk)

    def weighted_values(j):
        vblk = v_ref[pl.ds(pl.multiple_of(j * blk, blk), blk), :]
        vstack = jnp.concatenate([jnp.where(lane_head == h, vblk, jnp.zeros_like(vblk)) for h in range(nh)], axis=0)
        return jnp.dot(p_ref[...], vstack, preferred_element_type=F32)

    def weights(zs, sps, valid):
        probs, cars = _stick_weights(zs, sps, tri, [car_ref[h] for h in range(nh)], valid)
        for h in range(nh):
            p_ref[:, h * blk:(h + 1) * blk] = probs[h]
            car_ref[h] = cars[h]

    def store_scores(zs):
        for h in range(nh):
            z_ref[h] = zs[h]

    car_ref[...] = jnp.zeros_like(car_ref)
    store_scores(scores(i))
    z_next = scores(jnp.maximum(i - 1, 0))
    row = lax.broadcasted_iota(jnp.int32, (blk, blk), 0)
    col = lax.broadcasted_iota(jnp.int32, (blk, blk), 1)
    zs = [z_ref[h] for h in range(nh)]
    weights(zs, _softplus_bits(zs, col < row), col < row)
    store_scores(z_next)
    acc_ref[...] = jnp.zeros_like(acc_ref)

    def settled():
        return jnp.min(car_ref[...]) >= CARRY_SETTLED

    def body(state):
        t, _ = state
        j = i - t
        zs = [z_ref[h] for h in range(nh)]
        sps = _softplus_bits(zs, None)
        acc_ref[...] += weighted_values(j + 1)
        z_next = scores(jnp.maximum(j - 1, 0))
        weights(zs, sps, None)
        store_scores(z_next)
        return t + 1, settled()

    t_end, _ = lax.while_loop(lambda state: (state[0] <= i) & jnp.logical_not(state[1]), body,
                              (jnp.int32(1), settled()))
    o_ref[...] = (acc_ref[...] + weighted_values(i - t_end + 1)).astype(o_ref.dtype)


def _attn_prompt(q, kb, vb, tri, n_batch, t_pad):
    m = q.shape[0]
    nq = t_pad // ATT_BLOCK
    pair = ATT_HEADS * HEAD_DIM
    return pl.pallas_call(
        _attn_prompt_kernel,
        grid=(n_batch, N_HEADS // ATT_HEADS, nq),
        in_specs=[
            pl.BlockSpec((ATT_BLOCK, pair), lambda b, h, i: (b * nq + i, h)),
            pl.BlockSpec((t_pad, pair), lambda b, h, i: (b, h)),
            pl.BlockSpec((t_pad, pair), lambda b, h, i: (b, h)),
            pl.BlockSpec((ATT_BLOCK, ATT_BLOCK), lambda b, h, i: (0, 0)),
        ],
        out_specs=pl.BlockSpec((ATT_BLOCK, pair), lambda b, h, i: (b * nq + i, h)),
        out_shape=jax.ShapeDtypeStruct((m, D_ATTN), BF16),
        scratch_shapes=[pltpu.VMEM((ATT_BLOCK, pair), F32), pltpu.VMEM((ATT_HEADS, ATT_BLOCK, 1), F32),
                        pltpu.VMEM((ATT_HEADS, ATT_BLOCK, ATT_BLOCK), F32),
                        pltpu.VMEM((ATT_BLOCK, ATT_HEADS * ATT_BLOCK), BF16)],
        compiler_params=_params(("parallel", "parallel", "arbitrary")),
        name="attn_prompt",
    )(q, kb, vb, tri)


def _attn_sample_kernel(q_ref, kn_ref, vn_ref, kc_ref, vc_ref, tri_ref, oin_ref, o_ref, *, dec_seq, past_len):
    del oin_ref
    blk = ATT_BLOCK
    rows = N_HEADS * dec_seq
    q = q_ref[...]
    lane_head = lax.broadcasted_iota(jnp.int32, (dec_seq, D_ATTN), 1) // HEAD_DIM
    qs = jnp.concatenate([jnp.where(lane_head == h, q, jnp.zeros_like(q)) for h in range(N_HEADS)], axis=0)
    tri = tri_ref[...]

    pad = jnp.zeros((blk - dec_seq, D_ATTN), BF16)
    k_new = jnp.concatenate([kn_ref[...], pad], axis=0)
    v_new = jnp.concatenate([vn_ref[...], pad], axis=0)
    tq = lax.broadcasted_iota(jnp.int32, (rows, blk), 0) % dec_seq
    col = lax.broadcasted_iota(jnp.int32, (rows, blk), 1)
    zs = _scores([qs], k_new)
    (a,), (car,) = _stick_weights(zs, _softplus_bits(zs, col < tq), tri, [jnp.zeros((rows, 1), F32)], col < tq)
    acc = jnp.dot(a, v_new, preferred_element_type=F32)
    for j in reversed(range(past_len // blk)):
        kblk = kc_ref[0, 0, pl.ds(j * blk, blk), :].astype(BF16)
        vblk = vc_ref[0, 0, pl.ds(j * blk, blk), :].astype(BF16)
        zs = _scores([qs], kblk)
        (a,), (car,) = _stick_weights(zs, _softplus_bits(zs, None), tri, [car], None)
        acc = acc + jnp.dot(a, vblk, preferred_element_type=F32)
    o = jnp.zeros((dec_seq, D_ATTN), F32)
    for h in range(N_HEADS):
        o = o + jnp.where(lane_head == h, acc[h * dec_seq:(h + 1) * dec_seq, :], 0.0)
    o_ref[...] = o.astype(o_ref.dtype)


def _attn_sample(q, kb, vb, o_all, cache_k, cache_v, layer, tri, row0, n_streams, dec_seq):
    past_len = cache_k.shape[2]
    assert past_len % ATT_BLOCK == 0 and row0 % dec_seq == 0 and dec_seq % 16 == 0
    blk0 = row0 // dec_seq
    kern = functools.partial(_attn_sample_kernel, dec_seq=dec_seq, past_len=past_len)
    return pl.pallas_call(
        kern,
        grid=(n_streams,),
        in_specs=[
            pl.BlockSpec((dec_seq, D_ATTN), lambda b: (blk0 + b, 0)),
            pl.BlockSpec((dec_seq, D_ATTN), lambda b: (blk0 + b, 0)),
            pl.BlockSpec((dec_seq, D_ATTN), lambda b: (blk0 + b, 0)),
            pl.BlockSpec((1, 1, past_len, D_ATTN), lambda b: (layer, b, 0, 0)),
            pl.BlockSpec((1, 1, past_len, D_ATTN), lambda b: (layer, b, 0, 0)),
            pl.BlockSpec((ATT_BLOCK, ATT_BLOCK), lambda b: (0, 0)),
            pl.BlockSpec(memory_space=pl.ANY),
        ],
        out_specs=pl.BlockSpec((dec_seq, D_ATTN), lambda b: (blk0 + b, 0)),
        out_shape=jax.ShapeDtypeStruct(o_all.shape, o_all.dtype),
        input_output_aliases={6: 0},
        compiler_params=_params(("parallel",)),
        name="attn_sample",
    )(q, kb, vb, cache_k, cache_v, tri, o_all)


def _seqmix_kernel(ext_ref, cw_ref, pm_ref, cv_ref, cx_ref, *, pos0):
    ext = ext_ref[...]
    ext = jnp.concatenate([jnp.zeros((HALO, D_SEQ), F32), ext], axis=0)
    pos = jnp.full((ext.shape[0], 1), pos0, jnp.int32)
    pm, cv, cx = _pool_conv(ext, pos, cw_ref[...])
    pm_ref[...] = pm.astype(BF16)
    cv_ref[...] = cv.astype(BF16)
    cx_ref[...] = cx


def _seqmix_sample(ext, conv_w, pos0):
    rows = ext.shape[0]
    full = lambda w: pl.BlockSpec((rows, w), lambda: (0, 0))
    out = lambda dt: jax.ShapeDtypeStruct((rows, D_POOL), dt)
    return pl.pallas_call(
        functools.partial(_seqmix_kernel, pos0=pos0),
        in_specs=[full(D_SEQ), pl.BlockSpec((8, D_CONV), lambda: (0, 0))],
        out_specs=[full(D_POOL)] * 3,
        out_shape=[out(BF16), out(BF16), out(F32)],
        compiler_params=pltpu.CompilerParams(vmem_limit_bytes=VMEM_LIMIT),
        name="seqmix",
    )(ext, conv_w)


def _mixout_kernel(h_ref, g_ref, pm_ref, cv_ref, o_ref, wgrp_ref, ps_ref, pp_ref, cp_ref, ap_ref, wo_ref, out_ref):
    ya = jnp.dot(pm_ref[...], wgrp_ref[...], preferred_element_type=F32) * ps_ref[...]
    ya = jnp.dot(ya.astype(BF16), pp_ref[...], preferred_element_type=F32)
    yb = jnp.dot(cv_ref[...], cp_ref[...], preferred_element_type=F32)
    yc = jnp.dot(o_ref[...], ap_ref[...], preferred_element_type=F32)
    g = g_ref[...].astype(F32)
    mixed = g[:, 0:D_MODEL] * ya + g[:, D_MODEL:2 * D_MODEL] * yb + g[:, 2 * D_MODEL:3 * D_MODEL] * yc
    out_ref[...] = h_ref[...] + jnp.dot(mixed.astype(BF16), wo_ref[...], preferred_element_type=F32)


def _mixout(h, gates, pm, cv, o, wgrp, pscale, pool_proj, conv_proj, attn_proj, w_out):
    m = h.shape[0]
    row = lambda i: (i, 0)
    const = lambda i: (0, 0)
    return pl.pallas_call(
        _mixout_kernel,
        grid=(m // TM,),
        in_specs=[
            pl.BlockSpec((TM, D_MODEL), row),
            pl.BlockSpec((TM, N_GATE), row),
            pl.BlockSpec((TM, D_POOL), row),
            pl.BlockSpec((TM, D_CONV), row),
            pl.BlockSpec((TM, D_ATTN), row),
            pl.BlockSpec((D_POOL, D_POOL), const),
            pl.BlockSpec((1, D_POOL), const),
            pl.BlockSpec((D_POOL, D_MODEL), const),
            pl.BlockSpec((D_CONV, D_MODEL), const),
            pl.BlockSpec((D_ATTN, D_MODEL), const),
            pl.BlockSpec((D_MODEL, D_MODEL), const),
        ],
        out_specs=pl.BlockSpec((TM, D_MODEL), row),
        out_shape=jax.ShapeDtypeStruct((m, D_MODEL), F32),
        compiler_params=_params(("parallel",)),
        name="mixout",
    )(h, gates, pm, cv, o, wgrp, pscale, pool_proj, conv_proj, attn_proj, w_out)


def _block_diag(w):
    g, c, d = w.shape
    eye = jnp.eye(g, dtype=w.dtype)
    return (eye[:, None, :, None] * w[:, :, None, :]).reshape(g * c, g * d)


def kernel(x_prompt, x_sample, cache_k, cache_v, state_pool, state_conv, meta, ffn1_norm, ffn1_w_gate, ffn1_w_up, ffn1_w_down, mix_norm, w_in, pool_w, pool_scale, pool_proj, conv_w, conv_proj, q_norm, k_norm, attn_proj, w_out, ffn2_norm, ffn2_w_gate, ffn2_w_up, ffn2_w_down):
    n_batch, seq, _ = x_prompt.shape
    n_streams, dec_seq, _ = x_sample.shape
    depth = w_in.shape[0]
    past_len = cache_k.shape[2]
    t_real = N_META + seq
    t_pad = -(-t_real // ATT_BLOCK) * ATT_BLOCK
    rows_p = n_batch * t_pad
    rows_s = n_streams * dec_seq
    m_rows = rows_p + rows_s
    assert m_rows % TM == 0 and dec_seq > POOL_STATE and (HALO + dec_seq) % 8 == 0

    tail = jnp.zeros((t_pad - t_real, D_MODEL), F32)
    pieces = [piece for b in range(n_batch) for piece in (meta.astype(F32), x_prompt[b], tail)]
    h = jnp.concatenate(pieces + [x_sample.reshape(rows_s, D_MODEL)], axis=0)
    cache_k2 = cache_k.reshape(depth, n_streams, past_len, D_ATTN)
    cache_v2 = cache_v.reshape(depth, n_streams, past_len, D_ATTN)

    c_q = D_SEQ
    c_k, c_v, c_g = c_q + D_ATTN, c_q + 2 * D_ATTN, c_q + 3 * D_ATTN
    tri = (np.arange(ATT_BLOCK)[:, None] >= np.arange(ATT_BLOCK)[None, :])
    tri = jnp.asarray(tri, BF16)
    seg = jnp.asarray(np.kron(np.eye(N_HEADS), np.full((HEAD_DIM, HEAD_DIM), 1.0 / HEAD_DIM)), BF16)

    outs = [[] for _ in range(8)]
    for l in range(depth):
        h = _ffn(h, ffn1_norm[l][None], ffn1_w_gate[l].astype(BF16), ffn1_w_up[l].astype(BF16),
                 ffn1_w_down[l].astype(BF16))
        w = w_in[l].astype(BF16)
        cw = jnp.zeros((8, D_CONV), F32).at[:CONV_W].set(conv_w[l])
        gates, sq, q, k, kb, v, vb, pm, cv, cx_p = _proj(
            h, mix_norm[l][None], w[:, c_g:], w[:, :c_q], w[:, c_q:c_k], w[:, c_k:c_v], w[:, c_v:c_g],
            q_norm[l].reshape(1, D_ATTN), k_norm[l].reshape(1, D_ATTN), seg, cw, t_pad, t_real)

        o = _attn_prompt(q, kb, vb, tri, n_batch, t_pad)
        o = _attn_sample(q, kb, vb, o, cache_k2, cache_v2, l, tri, rows_p, n_streams, dec_seq)

        seq_s = sq[rows_p:].reshape(n_streams, dec_seq, D_SEQ)
        hist = jnp.zeros((n_streams, HALO, D_SEQ), F32)
        hist = hist.at[:, HALO - POOL_STATE:, 0:D_POOL].set(state_pool[l])
        hist = hist.at[:, HALO - (CONV_W - 1):, D_POOL:D_POOL + D_CONV].set(state_conv[l])
        hist = hist.at[:, :, D_POOL + 2 * D_CONV:].set(1.0)
        ext_s = jnp.concatenate([hist, seq_s], axis=1).reshape(n_streams * (HALO + dec_seq), D_SEQ)
        pm_s, cv_s, cx_s = _seqmix_sample(ext_s, cw, past_len)
        data = lambda t: t.reshape(n_streams, HALO + dec_seq, -1)[:, HALO:]
        pm = lax.dynamic_update_slice(pm, data(pm_s).reshape(rows_s, D_POOL), (rows_p, 0))
        cv = lax.dynamic_update_slice(cv, data(cv_s).reshape(rows_s, D_CONV), (rows_p, 0))

        h = _mixout(h, gates, pm, cv, o, _block_diag(pool_w[l]).astype(BF16), pool_scale[l][None],
                    pool_proj[l].astype(BF16), conv_proj[l].astype(BF16), attn_proj[l].astype(BF16),
                    w_out[l].astype(BF16))
        h = _ffn(h, ffn2_norm[l][None], ffn2_w_gate[l].astype(BF16), ffn2_w_up[l].astype(BF16),
                 ffn2_w_down[l].astype(BF16))

        prompt = lambda t, lo, hi: jnp.stack([t[b * t_pad + lo:b * t_pad + hi] for b in range(n_batch)])
        sample = lambda t: t[rows_p:].reshape(n_streams, dec_seq, -1)
        heads = lambda t: t.reshape(t.shape[0], t.shape[1], N_HEADS, HEAD_DIM)
        outs[0].append(heads(prompt(k, 0, t_real)))
        outs[1].append(heads(prompt(v, 0, t_real)))
        outs[2].append(prompt(sq, t_real - POOL_STATE, t_real)[:, :, :D_POOL])
        outs[3].append(prompt(cx_p, t_real - (CONV_W - 1), t_real))
        outs[4].append(heads(sample(k)))
        outs[5].append(heads(sample(v)))
        outs[6].append(sample(sq)[:, dec_seq - POOL_STATE:, :D_POOL])
        outs[7].append(data(cx_s)[:, dec_seq - (CONV_W - 1):])

    y_prompt = jnp.stack([h[b * t_pad + N_META:b * t_pad + t_real] for b in range(n_batch)])
    y_sample = h[rows_p:].reshape(n_streams, dec_seq, D_MODEL)
    return (y_prompt, y_sample) + tuple(jnp.stack(o) for o in outs)
```

```python
import functools

import jax
import jax.numpy as jnp
import numpy as np
from jax import lax
from jax.experimental import pallas as pl
from jax.experimental.pallas import tpu as pltpu

F32 = jnp.float32
BF16 = jnp.bfloat16

D_MODEL = 1024
N_META = 16
D_POOL = 256
POOL_WINDOWS = (2, 4, 8, 16)
POOL_GROUP = 64
POOL_STATE = 15
D_CONV = 256
CONV_W = 3
N_HEADS = 8
HEAD_DIM = 64
D_ATTN = N_HEADS * HEAD_DIM
D_FF = 2816
RMS_EPS = 1e-6
LOG2E = 1.4426950408889634
Q_SCALE = HEAD_DIM ** -0.5 * LOG2E
SP_CLAMP = 64.0
CARRY_SETTLED = 160.0
N_GATE = 3 * D_MODEL
D_SEQ = D_POOL + 3 * D_CONV

ATT_BLOCK = 256
ATT_HEADS = 4
HALO = 16
TM = 512
PROJ_CHUNK = 512
VMEM_LIMIT = 56 * 1024 * 1024


def _params(sem):
    return pltpu.CompilerParams(dimension_semantics=sem, vmem_limit_bytes=VMEM_LIMIT)


def _rms_rows(x, gain):
    ms = jnp.mean(x * x, axis=-1, keepdims=True)
    return x * lax.rsqrt(ms + RMS_EPS) * gain


def _ffn_kernel(x_ref, g_ref, wg_ref, wu_ref, wd_ref, o_ref):
    x = x_ref[...]
    xn = _rms_rows(x, g_ref[...]).astype(BF16)
    g = jnp.dot(xn, wg_ref[...], preferred_element_type=F32)
    u = jnp.dot(xn, wu_ref[...], preferred_element_type=F32)
    a = (g * jax.nn.sigmoid(g) * u).astype(BF16)
    o_ref[...] = x + 0.5 * jnp.dot(a, wd_ref[...], preferred_element_type=F32)


def _resident(shape):
    return pl.BlockSpec(shape, lambda *_: (0,) * len(shape), pipeline_mode=pl.Buffered(1))


def _ffn(x, gain, wg, wu, wd):
    m = x.shape[0]
    return pl.pallas_call(
        _ffn_kernel,
        grid=(m // TM,),
        in_specs=[
            pl.BlockSpec((TM, D_MODEL), lambda i: (i, 0)),
            pl.BlockSpec((1, D_MODEL), lambda i: (0, 0)),
            _resident((D_MODEL, D_FF)),
            _resident((D_MODEL, D_FF)),
            _resident((D_FF, D_MODEL)),
        ],
        out_specs=pl.BlockSpec((TM, D_MODEL), lambda i: (i, 0)),
        out_shape=jax.ShapeDtypeStruct((m, D_MODEL), F32),
        compiler_params=_params(("parallel",)),
        name="ffn",
    )(x, gain, wg, wu, wd)


def _pool_conv(ext, pos, cw):
    a = ext[:, 0:D_POOL]
    cx = ext[:, D_POOL + 2 * D_CONV:D_POOL + 3 * D_CONV] * ext[:, D_POOL:D_POOL + D_CONV]
    sums = []
    s = a
    for w in POOL_WINDOWS:
        s = s + pltpu.roll(s, w // 2, 0)
        sums.append(s)
    lane_group = lax.broadcasted_iota(jnp.int32, a.shape, 1) // POOL_GROUP
    mean = jnp.zeros_like(a)
    for g, w in enumerate(POOL_WINDOWS):
        cnt = jnp.clip(pos + 1, 1, w).astype(F32)
        mean = jnp.where(lane_group == g, sums[g] / cnt, mean)
    y = pltpu.roll(cx, 2, 0) * cw[0:1] + pltpu.roll(cx, 1, 0) * cw[1:2] + cx * cw[2:3]
    gated = ext[:, D_POOL + D_CONV:D_POOL + 2 * D_CONV] * y
    return (mean - a)[HALO:], gated[HALO:], cx[HALO:]


def _head_rms(p, seg, gain):
    pp = p * p
    hi = pp.astype(BF16)
    lo = (pp - hi.astype(F32)).astype(BF16)
    ms = jnp.dot(hi, seg, preferred_element_type=F32) + jnp.dot(lo, seg, preferred_element_type=F32)
    return p * lax.rsqrt(ms + RMS_EPS) * gain


def _proj_kernel(h_ref, g_ref, wg_ref, ws_ref, wq_ref, wk_ref, wv_ref, qn_ref, kn_ref, seg_ref, cw_ref,
                 gate_ref, seq_ref, q_ref, k_ref, kb_ref, v_ref, vb_ref, pm_ref, cv_ref, cx_ref, halo_ref,
                 *, t_pad, t_real):
    i = pl.program_id(0)

    @pl.when(i == 0)
    def _():
        halo_ref[...] = jnp.zeros_like(halo_ref)

    u = _rms_rows(h_ref[...], g_ref[...]).astype(BF16)
    ch = PROJ_CHUNK
    for c in range(N_GATE // ch):
        p = jnp.dot(u, wg_ref[:, c * ch:(c + 1) * ch], preferred_element_type=F32)
        gate_ref[:, c * ch:(c + 1) * ch] = jax.nn.sigmoid(p).astype(BF16)
    for c in range(D_SEQ // ch):
        seq_ref[:, c * ch:(c + 1) * ch] = jnp.dot(u, ws_ref[:, c * ch:(c + 1) * ch], preferred_element_type=F32)
    q = _head_rms(jnp.dot(u, wq_ref[...], preferred_element_type=F32), seg_ref[...], qn_ref[...])
    q_ref[...] = (q * Q_SCALE).astype(BF16)
    k = _head_rms(jnp.dot(u, wk_ref[...], preferred_element_type=F32), seg_ref[...], kn_ref[...])
    k_ref[...] = k
    kb_ref[...] = k.astype(BF16)
    v = jnp.dot(u, wv_ref[...], preferred_element_type=F32)
    v_ref[...] = v
    vb_ref[...] = v.astype(BF16)

    tm = seq_ref.shape[0]
    row0 = i * tm
    pos0 = row0 - (row0 // t_pad) * t_pad
    pos = pos0 + lax.broadcasted_iota(jnp.int32, (tm, 1), 0)
    pos = jnp.where(pos >= t_pad, pos - t_pad, pos)
    seq = jnp.where(pos < t_real, seq_ref[...], 0.0)
    ext = jnp.concatenate([halo_ref[...], seq], axis=0)
    pos_ext = jnp.concatenate([jnp.zeros((HALO, 1), jnp.int32), pos], axis=0)
    pm, cv, cx = _pool_conv(ext, pos_ext, cw_ref[...])
    pm_ref[...] = pm.astype(BF16)
    cv_ref[...] = cv.astype(BF16)
    cx_ref[...] = cx
    halo_ref[...] = seq[tm - HALO:]


def _proj(h, gain, wg, ws, wq, wk, wv, qn, kn, seg, conv_w, t_pad, t_real):
    m = h.shape[0]
    assert TM <= t_pad and t_pad - t_real >= HALO
    row = lambda i: (i, 0)
    const = lambda i: (0, 0)
    widths = [(N_GATE, BF16), (D_SEQ, F32), (D_ATTN, BF16), (D_ATTN, F32), (D_ATTN, BF16), (D_ATTN, F32), (D_ATTN, BF16),
              (D_POOL, BF16), (D_CONV, BF16), (D_CONV, F32)]
    return pl.pallas_call(
        functools.partial(_proj_kernel, t_pad=t_pad, t_real=t_real),
        grid=(m // TM,),
        in_specs=[
            pl.BlockSpec((TM, D_MODEL), row),
            pl.BlockSpec((1, D_MODEL), const),
            _resident((D_MODEL, N_GATE)),
            _resident((D_MODEL, D_SEQ)),
            _resident((D_MODEL, D_ATTN)),
            _resident((D_MODEL, D_ATTN)),
            _resident((D_MODEL, D_ATTN)),
            pl.BlockSpec((1, D_ATTN), const),
            pl.BlockSpec((1, D_ATTN), const),
            _resident((D_ATTN, D_ATTN)),
            pl.BlockSpec((8, D_CONV), const),
        ],
        out_specs=[pl.BlockSpec((TM, w), row) for w, _ in widths],
        out_shape=[jax.ShapeDtypeStruct((m, w), dt) for w, dt in widths],
        scratch_shapes=[pltpu.VMEM((HALO, D_SEQ), F32)],
        compiler_params=_params(("arbitrary",)),
        name="proj",
    )(h, gain, wg, ws, wq, wk, wv, qn, kn, seg, conv_w)


def _scores(qs, kblk):
    rows = qs[0].shape[0]
    z = lax.dot_general(jnp.concatenate(qs, axis=0), kblk, (((1,), (1,)), ((), ())), preferred_element_type=F32)
    return [z[h * rows:(h + 1) * rows] for h in range(len(qs))]


def _softplus_bits(zs, valid):
    out = []
    for z in zs:
        sp = jnp.maximum(jnp.log(1.0 + jnp.exp2(jnp.minimum(z, SP_CLAMP))) * LOG2E, z)
        if valid is not None:
            sp = jnp.where(valid, sp, 0.0)
        out.append(sp.astype(BF16))
    return out


def _stick_weights(zs, sps, tri, carries, valid):
    cums = [jnp.dot(sp, tri, preferred_element_type=F32) for sp in sps]
    probs = []
    for z, cum, carry in zip(zs, cums, carries):
        a = jnp.exp2(z - cum - carry)
        if valid is not None:
            a = jnp.where(valid, a, 0.0)
        probs.append(a.astype(BF16))
    return probs, [carry + cum[:, 0:1] for carry, cum in zip(carries, cums)]


def _attn_prompt_kernel(q_ref, k_ref, v_ref, tri_ref, o_ref, acc_ref, car_ref, z_ref, p_ref):
    i = pl.program_id(2)
    blk = ATT_BLOCK
    nh = ATT_HEADS
    lane_head = lax.broadcasted_iota(jnp.int32, (blk, nh * HEAD_DIM), 1) // HEAD_DIM
    q2 = q_ref[...]
    qs = [jnp.where(lane_head == h, q2, jnp.zeros_like(q2)) for h in range(nh)]
    tri = tri_ref[...]

    def scores(j):
        kblk = k_ref[pl.ds(pl.multiple_of(j * blk, blk), blk), :]
        return _scores(qs, kblk)

    def weighted_values(j):
        vblk = v_ref[pl.ds(pl.multiple_of(j * blk, blk), blk), :]
        vstack = jnp.concatenate([jnp.where(lane_head == h, vblk, jnp.zeros_like(vblk)) for h in range(nh)], axis=0)
        return jnp.dot(p_ref[...], vstack, preferred_element_type=F32)

    def weights(zs, sps, valid):
        probs, cars = _stick_weights(zs, sps, tri, [car_ref[h] for h in range(nh)], valid)
        for h in range(nh):
            p_ref[:, h * blk:(h + 1) * blk] = probs[h]
            car_ref[h] = cars[h]

    def store_scores(zs):
        for h in range(nh):
            z_ref[h] = zs[h]

    car_ref[...] = jnp.zeros_like(car_ref)
    store_scores(scores(i))
    z_next = scores(jnp.maximum(i - 1, 0))
    row = lax.broadcasted_iota(jnp.int32, (blk, blk), 0)
    col = lax.broadcasted_iota(jnp.int32, (blk, blk), 1)
    zs = [z_ref[h] for h in range(nh)]
    weights(zs, _softplus_bits(zs, col < row), col < row)
    store_scores(z_next)
    acc_ref[...] = jnp.zeros_like(acc_ref)

    def settled():
        return jnp.min(car_ref[...]) >= CARRY_SETTLED

    def body(state):
        t, _ = state
        j = i - t
        zs = [z_ref[h] for h in range(nh)]
        sps = _softplus_bits(zs, None)
        acc_ref[...] += weighted_values(j + 1)
        z_next = scores(jnp.maximum(j - 1, 0))
        weights(zs, sps, None)
        store_scores(z_next)
        return t + 1, settled()

    t_end, _ = lax.while_loop(lambda state: (state[0] <= i) & jnp.logical_not(state[1]), body,
                              (jnp.int32(1), settled()))
    o_ref[...] = (acc_ref[...] + weighted_values(i - t_end + 1)).astype(o_ref.dtype)


def _attn_prompt(q, kb, vb, tri, n_batch, t_pad):
    nq = t_pad // ATT_BLOCK
    pair = ATT_HEADS * HEAD_DIM
    return pl.pallas_call(
        _attn_prompt_kernel,
        grid=(n_batch, N_HEADS // ATT_HEADS, nq),
        in_specs=[
            pl.BlockSpec((ATT_BLOCK, pair), lambda b, h, i: (b * nq + i, h)),
            pl.BlockSpec((t_pad, pair), lambda b, h, i: (b, h)),
            pl.BlockSpec((t_pad, pair), lambda b, h, i: (b, h)),
            pl.BlockSpec((ATT_BLOCK, ATT_BLOCK), lambda b, h, i: (0, 0)),
        ],
        out_specs=pl.BlockSpec((ATT_BLOCK, pair), lambda b, h, i: (b * nq + i, h)),
        out_shape=jax.ShapeDtypeStruct((n_batch * t_pad, D_ATTN), BF16),
        scratch_shapes=[pltpu.VMEM((ATT_BLOCK, pair), F32), pltpu.VMEM((ATT_HEADS, ATT_BLOCK, 1), F32),
                        pltpu.VMEM((ATT_HEADS, ATT_BLOCK, ATT_BLOCK), F32),
                        pltpu.VMEM((ATT_BLOCK, ATT_HEADS * ATT_BLOCK), BF16)],
        compiler_params=_params(("parallel", "parallel", "arbitrary")),
        name="attn_prompt",
    )(q, kb, vb, tri)


def _attn_sample_kernel(*refs, dec_seq, n_blocks, first):
    blk = ATT_BLOCK
    rows = N_HEADS * dec_seq
    if first:
        q_ref, kn_ref, vn_ref, kc_ref, vc_ref, tri_ref, o_ref, acc_ref, car_ref = refs
    else:
        q_ref, kc_ref, vc_ref, tri_ref, acc_in_ref, car_in_ref, o_ref, acc_ref, car_ref = refs
    q = q_ref[...]
    lane_head = lax.broadcasted_iota(jnp.int32, (dec_seq, D_ATTN), 1) // HEAD_DIM
    qs = jnp.concatenate([jnp.where(lane_head == h, q, jnp.zeros_like(q)) for h in range(N_HEADS)], axis=0)
    tri = tri_ref[...]

    if first:
        pad = jnp.zeros((blk - dec_seq, D_ATTN), BF16)
        k_new = jnp.concatenate([kn_ref[...], pad], axis=0)
        v_new = jnp.concatenate([vn_ref[...], pad], axis=0)
        tq = lax.broadcasted_iota(jnp.int32, (rows, blk), 0) % dec_seq
        col = lax.broadcasted_iota(jnp.int32, (rows, blk), 1)
        zs = _scores([qs], k_new)
        (a,), (car,) = _stick_weights(zs, _softplus_bits(zs, col < tq), tri, [jnp.zeros((rows, 1), F32)], col < tq)
        acc = jnp.dot(a, v_new, preferred_element_type=F32)
    else:
        acc, car = acc_in_ref[...], car_in_ref[...]
    for j in reversed(range(n_blocks)):
        kblk = kc_ref[0, pl.ds(j * blk, blk), :].astype(BF16)
        vblk = vc_ref[0, pl.ds(j * blk, blk), :].astype(BF16)
        zs = _scores([qs], kblk)
        (a,), (car,) = _stick_weights(zs, _softplus_bits(zs, None), tri, [car], None)
        acc = acc + jnp.dot(a, vblk, preferred_element_type=F32)
    acc_ref[...] = acc
    car_ref[...] = car
    o = jnp.zeros((dec_seq, D_ATTN), F32)
    for h in range(N_HEADS):
        o = o + jnp.where(lane_head == h, acc[h * dec_seq:(h + 1) * dec_seq, :], 0.0)
    o_ref[...] = o.astype(o_ref.dtype)


def _attn_sample_call(q, new_kv, cache_k, cache_v, tri, state, row0, n_streams, dec_seq):
    n = cache_k.shape[1]
    assert n % ATT_BLOCK == 0 and row0 % dec_seq == 0 and dec_seq % 16 == 0
    blk0 = row0 // dec_seq
    rows = N_HEADS * dec_seq
    new_rows = pl.BlockSpec((dec_seq, D_ATTN), lambda b: (blk0 + b, 0))
    cache = pl.BlockSpec((1, n, D_ATTN), lambda b: (b, 0, 0))
    tri_spec = pl.BlockSpec((ATT_BLOCK, ATT_BLOCK), lambda b: (0, 0))
    acc_spec = pl.BlockSpec((rows, D_ATTN), lambda b: (b, 0))
    car_spec = pl.BlockSpec((rows, 1), lambda b: (b, 0))
    if state is None:
        operands = (q,) + tuple(new_kv) + (cache_k, cache_v, tri)
        in_specs = [new_rows, new_rows, new_rows, cache, cache, tri_spec]
    else:
        operands = (q, cache_k, cache_v, tri) + tuple(state)
        in_specs = [new_rows, cache, cache, tri_spec, acc_spec, car_spec]
    return pl.pallas_call(
        functools.partial(_attn_sample_kernel, dec_seq=dec_seq, n_blocks=n // ATT_BLOCK, first=state is None),
        grid=(n_streams,),
        in_specs=in_specs,
        out_specs=[pl.BlockSpec((dec_seq, D_ATTN), lambda b: (b, 0)), acc_spec, car_spec],
        out_shape=[jax.ShapeDtypeStruct((n_streams * dec_seq, D_ATTN), BF16),
                   jax.ShapeDtypeStruct((n_streams * rows, D_ATTN), F32),
                   jax.ShapeDtypeStruct((n_streams * rows, 1), F32)],
        compiler_params=_params(("parallel",)),
        name="attn_sample",
    )(*operands)


def _attn_sample(q, kb, vb, cache_k, cache_v, tri, row0, n_streams, dec_seq):
    past_len = cache_k.shape[1]
    split = past_len - ATT_BLOCK
    flat = lambda c: c.reshape(n_streams, c.shape[1], D_ATTN)
    o, acc, car = _attn_sample_call(q, (kb, vb), flat(cache_k[:, split:]), flat(cache_v[:, split:]), tri, None,
                                    row0, n_streams, dec_seq)
    if split == 0:
        return o

    def earlier_keys(_):
        return _attn_sample_call(q, None, flat(cache_k[:, :split]), flat(cache_v[:, :split]), tri, (acc, car),
                                 row0, n_streams, dec_seq)[0]

    return lax.cond(jnp.min(car) >= CARRY_SETTLED, lambda _: o, earlier_keys, None)


def _seqmix_kernel(ext_ref, cw_ref, pm_ref, cv_ref, cx_ref, *, pos0):
    ext = ext_ref[...]
    ext = jnp.concatenate([jnp.zeros((HALO, D_SEQ), F32), ext], axis=0)
    pos = jnp.full((ext.shape[0], 1), pos0, jnp.int32)
    pm, cv, cx = _pool_conv(ext, pos, cw_ref[...])
    pm_ref[...] = pm.astype(BF16)
    cv_ref[...] = cv.astype(BF16)
    cx_ref[...] = cx


def _seqmix_sample(ext, conv_w, pos0):
    rows = ext.shape[0]
    full = lambda w: pl.BlockSpec((rows, w), lambda: (0, 0))
    out = lambda dt: jax.ShapeDtypeStruct((rows, D_POOL), dt)
    return pl.pallas_call(
        functools.partial(_seqmix_kernel, pos0=pos0),
        in_specs=[full(D_SEQ), pl.BlockSpec((8, D_CONV), lambda: (0, 0))],
        out_specs=[full(D_POOL)] * 3,
        out_shape=[out(BF16), out(BF16), out(F32)],
        compiler_params=pltpu.CompilerParams(vmem_limit_bytes=VMEM_LIMIT),
        name="seqmix",
    )(ext, conv_w)


def _mixout_kernel(h_ref, g_ref, pm_ref, cv_ref, op_ref, os_ref, wgrp_ref, ps_ref, pp_ref, cp_ref, ap_ref, wo_ref,
                   out_ref):
    o = jnp.where(pl.program_id(0) == pl.num_programs(0) - 1, os_ref[...], op_ref[...])
    ya = jnp.dot(pm_ref[...], wgrp_ref[...], preferred_element_type=F32) * ps_ref[...]
    ya = jnp.dot(ya.astype(BF16), pp_ref[...], preferred_element_type=F32)
    yb = jnp.dot(cv_ref[...], cp_ref[...], preferred_element_type=F32)
    yc = jnp.dot(o, ap_ref[...], preferred_element_type=F32)
    g = g_ref[...].astype(F32)
    mixed = g[:, 0:D_MODEL] * ya + g[:, D_MODEL:2 * D_MODEL] * yb + g[:, 2 * D_MODEL:3 * D_MODEL] * yc
    out_ref[...] = h_ref[...] + jnp.dot(mixed.astype(BF16), wo_ref[...], preferred_element_type=F32)


def _mixout(h, gates, pm, cv, o_prompt, o_sample, wgrp, pscale, pool_proj, conv_proj, attn_proj, w_out):
    m = h.shape[0]
    n_prompt_tiles = o_prompt.shape[0] // TM
    assert o_prompt.shape[0] % TM == 0 and o_sample.shape[0] == TM and m == o_prompt.shape[0] + TM
    row = lambda i: (i, 0)
    const = lambda i: (0, 0)
    return pl.pallas_call(
        _mixout_kernel,
        grid=(m // TM,),
        in_specs=[
            pl.BlockSpec((TM, D_MODEL), row),
            pl.BlockSpec((TM, N_GATE), row),
            pl.BlockSpec((TM, D_POOL), row),
            pl.BlockSpec((TM, D_CONV), row),
            pl.BlockSpec((TM, D_ATTN), lambda i: (jnp.minimum(i, n_prompt_tiles - 1), 0)),
            pl.BlockSpec((TM, D_ATTN), const),
            pl.BlockSpec((D_POOL, D_POOL), const),
            pl.BlockSpec((1, D_POOL), const),
            pl.BlockSpec((D_POOL, D_MODEL), const),
            pl.BlockSpec((D_CONV, D_MODEL), const),
            pl.BlockSpec((D_ATTN, D_MODEL), const),
            pl.BlockSpec((D_MODEL, D_MODEL), const),
        ],
        out_specs=pl.BlockSpec((TM, D_MODEL), row),
        out_shape=jax.ShapeDtypeStruct((m, D_MODEL), F32),
        compiler_params=_params(("parallel",)),
        name="mixout",
    )(h, gates, pm, cv, o_prompt, o_sample, wgrp, pscale, pool_proj, conv_proj, attn_proj, w_out)


def _block_diag(w):
    g, c, d = w.shape
    eye = jnp.eye(g, dtype=w.dtype)
    return (eye[:, None, :, None] * w[:, :, None, :]).reshape(g * c, g * d)


def kernel(x_prompt, x_sample, cache_k, cache_v, state_pool, state_conv, meta, ffn1_norm, ffn1_w_gate, ffn1_w_up, ffn1_w_down, mix_norm, w_in, pool_w, pool_scale, pool_proj, conv_w, conv_proj, q_norm, k_norm, attn_proj, w_out, ffn2_norm, ffn2_w_gate, ffn2_w_up, ffn2_w_down):
    n_batch, seq, _ = x_prompt.shape
    n_streams, dec_seq, _ = x_sample.shape
    depth = w_in.shape[0]
    past_len = cache_k.shape[2]
    t_real = N_META + seq
    t_pad = -(-t_real // ATT_BLOCK) * ATT_BLOCK
    rows_p = n_batch * t_pad
    rows_s = n_streams * dec_seq
    m_rows = rows_p + rows_s
    assert m_rows % TM == 0 and dec_seq > POOL_STATE and (HALO + dec_seq) % 8 == 0

    tail = jnp.zeros((t_pad - t_real, D_MODEL), F32)
    pieces = [piece for b in range(n_batch) for piece in (meta.astype(F32), x_prompt[b], tail)]
    h = jnp.concatenate(pieces + [x_sample.reshape(rows_s, D_MODEL)], axis=0)

    c_q = D_SEQ
    c_k, c_v, c_g = c_q + D_ATTN, c_q + 2 * D_ATTN, c_q + 3 * D_ATTN
    tri = (np.arange(ATT_BLOCK)[:, None] >= np.arange(ATT_BLOCK)[None, :])
    tri = jnp.asarray(tri, BF16)
    seg = jnp.asarray(np.kron(np.eye(N_HEADS), np.full((HEAD_DIM, HEAD_DIM), 1.0 / HEAD_DIM)), BF16)

    outs = [[] for _ in range(8)]
    for l in range(depth):
        h = _ffn(h, ffn1_norm[l][None], ffn1_w_gate[l].astype(BF16), ffn1_w_up[l].astype(BF16),
                 ffn1_w_down[l].astype(BF16))
        w = w_in[l].astype(BF16)
        cw = jnp.zeros((8, D_CONV), F32).at[:CONV_W].set(conv_w[l])
        gates, sq, q, k, kb, v, vb, pm, cv, cx_p = _proj(
            h, mix_norm[l][None], w[:, c_g:], w[:, :c_q], w[:, c_q:c_k], w[:, c_k:c_v], w[:, c_v:c_g],
            q_norm[l].reshape(1, D_ATTN), k_norm[l].reshape(1, D_ATTN), seg, cw, t_pad, t_real)

        o_prompt = _attn_prompt(q, kb, vb, tri, n_batch, t_pad)
        o_sample = _attn_sample(q, kb, vb, cache_k[l], cache_v[l], tri, rows_p, n_streams, dec_seq)

        seq_s = sq[rows_p:].reshape(n_streams, dec_seq, D_SEQ)
        hist = jnp.zeros((n_streams, HALO, D_SEQ), F32)
        hist = hist.at[:, HALO - POOL_STATE:, 0:D_POOL].set(state_pool[l])
        hist = hist.at[:, HALO - (CONV_W - 1):, D_POOL:D_POOL + D_CONV].set(state_conv[l])
        hist = hist.at[:, :, D_POOL + 2 * D_CONV:].set(1.0)
        ext_s = jnp.concatenate([hist, seq_s], axis=1).reshape(n_streams * (HALO + dec_seq), D_SEQ)
        pm_s, cv_s, cx_s = _seqmix_sample(ext_s, cw, past_len)
        data = lambda t: t.reshape(n_streams, HALO + dec_seq, -1)[:, HALO:]
        pm = lax.dynamic_update_slice(pm, data(pm_s).reshape(rows_s, D_POOL), (rows_p, 0))
        cv = lax.dynamic_update_slice(cv, data(cv_s).reshape(rows_s, D_CONV), (rows_p, 0))

        h = _mixout(h, gates, pm, cv, o_prompt, o_sample, _block_diag(pool_w[l]).astype(BF16), pool_scale[l][None],
                    pool_proj[l].astype(BF16), conv_proj[l].astype(BF16), attn_proj[l].astype(BF16),
                    w_out[l].astype(BF16))
        h = _ffn(h, ffn2_norm[l][None], ffn2_w_gate[l].astype(BF16), ffn2_w_up[l].astype(BF16),
                 ffn2_w_down[l].astype(BF16))

        prompt = lambda t, lo, hi: jnp.stack([t[b * t_pad + lo:b * t_pad + hi] for b in range(n_batch)])
        sample = lambda t: t[rows_p:].reshape(n_streams, dec_seq, -1)
        heads = lambda t: t.reshape(t.shape[0], t.shape[1], N_HEADS, HEAD_DIM)
        outs[0].append(heads(prompt(k, 0, t_real)))
        outs[1].append(heads(prompt(v, 0, t_real)))
        outs[2].append(prompt(sq, t_real - POOL_STATE, t_real)[:, :, :D_POOL])
        outs[3].append(prompt(cx_p, t_real - (CONV_W - 1), t_real))
        outs[4].append(heads(sample(k)))
        outs[5].append(heads(sample(v)))
        outs[6].append(sample(sq)[:, dec_seq - POOL_STATE:, :D_POOL])
        outs[7].append(data(cx_s)[:, dec_seq - (CONV_W - 1):])

    y_prompt = jnp.stack([h[b * t_pad + N_META:b * t_pad + t_real] for b in range(n_batch)])
    y_sample = h[rows_p:].reshape(n_streams, dec_seq, D_MODEL)
    return (y_prompt, y_sample) + tuple(jnp.stack(o) for o in outs)
```

```python
import functools

import jax
import jax.numpy as jnp
import numpy as np
from jax import lax
from jax.experimental import pallas as pl
from jax.experimental.pallas import tpu as pltpu

F32 = jnp.float32
BF16 = jnp.bfloat16

D_MODEL = 1024
N_META = 16
D_POOL = 256
POOL_WINDOWS = (2, 4, 8, 16)
POOL_GROUP = 64
POOL_STATE = 15
D_CONV = 256
CONV_W = 3
N_HEADS = 8
HEAD_DIM = 64
D_ATTN = N_HEADS * HEAD_DIM
D_FF = 2816
RMS_EPS = 1e-6
LOG2E = 1.4426950408889634
Q_SCALE = HEAD_DIM ** -0.5 * LOG2E
SP_CLAMP = 64.0
CARRY_SETTLED = 160.0
N_GATE = 3 * D_MODEL
D_SEQ = D_POOL + 3 * D_CONV

ATT_BLOCK = 256
ATT_HEADS = 4
HALO = 16
TM = 512
PROJ_CHUNK = 512
VMEM_LIMIT = 56 * 1024 * 1024


def _params(sem):
    return pltpu.CompilerParams(dimension_semantics=sem, vmem_limit_bytes=VMEM_LIMIT)


def _rms_rows(x, gain):
    ms = jnp.mean(x * x, axis=-1, keepdims=True)
    return x * lax.rsqrt(ms + RMS_EPS) * gain


def _ffn_kernel(x_ref, g_ref, wg_ref, wu_ref, wd_ref, o_ref):
    x = x_ref[...]
    xn = _rms_rows(x, g_ref[...]).astype(BF16)
    g = jnp.dot(xn, wg_ref[...], preferred_element_type=F32)
    u = jnp.dot(xn, wu_ref[...], preferred_element_type=F32)
    a = (g * jax.nn.sigmoid(g) * u).astype(BF16)
    o_ref[...] = x + 0.5 * jnp.dot(a, wd_ref[...], preferred_element_type=F32)


def _resident(shape):
    return pl.BlockSpec(shape, lambda *_: (0,) * len(shape), pipeline_mode=pl.Buffered(1))


def _ffn(x, gain, wg, wu, wd):
    m = x.shape[0]
    return pl.pallas_call(
        _ffn_kernel,
        grid=(m // TM,),
        in_specs=[
            pl.BlockSpec((TM, D_MODEL), lambda i: (i, 0)),
            pl.BlockSpec((1, D_MODEL), lambda i: (0, 0)),
            _resident((D_MODEL, D_FF)),
            _resident((D_MODEL, D_FF)),
            _resident((D_FF, D_MODEL)),
        ],
        out_specs=pl.BlockSpec((TM, D_MODEL), lambda i: (i, 0)),
        out_shape=jax.ShapeDtypeStruct((m, D_MODEL), F32),
        compiler_params=_params(("parallel",)),
        name="ffn",
    )(x, gain, wg, wu, wd)


def _pool_conv(ext, pos, cw):
    a = ext[:, 0:D_POOL]
    cx = ext[:, D_POOL + 2 * D_CONV:D_POOL + 3 * D_CONV] * ext[:, D_POOL:D_POOL + D_CONV]
    sums = []
    s = a
    for w in POOL_WINDOWS:
        s = s + pltpu.roll(s, w // 2, 0)
        sums.append(s)
    lane_group = lax.broadcasted_iota(jnp.int32, a.shape, 1) // POOL_GROUP
    mean = jnp.zeros_like(a)
    for g, w in enumerate(POOL_WINDOWS):
        cnt = jnp.clip(pos + 1, 1, w).astype(F32)
        mean = jnp.where(lane_group == g, sums[g] / cnt, mean)
    y = pltpu.roll(cx, 2, 0) * cw[0:1] + pltpu.roll(cx, 1, 0) * cw[1:2] + cx * cw[2:3]
    gated = ext[:, D_POOL + D_CONV:D_POOL + 2 * D_CONV] * y
    return (mean - a)[HALO:], gated[HALO:], cx[HALO:]


def _head_rms(p, seg, gain):
    pp = p * p
    hi = pp.astype(BF16)
    lo = (pp - hi.astype(F32)).astype(BF16)
    ms = jnp.dot(hi, seg, preferred_element_type=F32) + jnp.dot(lo, seg, preferred_element_type=F32)
    return p * lax.rsqrt(ms + RMS_EPS) * gain


def _proj_kernel(h_ref, g_ref, wg_ref, ws_ref, wq_ref, wk_ref, wv_ref, qn_ref, kn_ref, seg_ref, cw_ref,
                 gate_ref, seq_ref, q_ref, k_ref, kb_ref, v_ref, vb_ref, pm_ref, cv_ref, cx_ref, halo_ref,
                 *, t_pad, t_real):
    i = pl.program_id(0)

    @pl.when(i == 0)
    def _():
        halo_ref[...] = jnp.zeros_like(halo_ref)

    u = _rms_rows(h_ref[...], g_ref[...]).astype(BF16)
    ch = PROJ_CHUNK
    for c in range(N_GATE // ch):
        p = jnp.dot(u, wg_ref[:, c * ch:(c + 1) * ch], preferred_element_type=F32)
        gate_ref[:, c * ch:(c + 1) * ch] = jax.nn.sigmoid(p).astype(BF16)
    for c in range(D_SEQ // ch):
        seq_ref[:, c * ch:(c + 1) * ch] = jnp.dot(u, ws_ref[:, c * ch:(c + 1) * ch], preferred_element_type=F32)
    q = _head_rms(jnp.dot(u, wq_ref[...], preferred_element_type=F32), seg_ref[...], qn_ref[...])
    q_ref[...] = (q * Q_SCALE).astype(BF16)
    k = _head_rms(jnp.dot(u, wk_ref[...], preferred_element_type=F32), seg_ref[...], kn_ref[...])
    k_ref[...] = k
    kb_ref[...] = k.astype(BF16)
    v = jnp.dot(u, wv_ref[...], preferred_element_type=F32)
    v_ref[...] = v
    vb_ref[...] = v.astype(BF16)

    tm = seq_ref.shape[0]
    row0 = i * tm
    pos0 = row0 - (row0 // t_pad) * t_pad
    pos = pos0 + lax.broadcasted_iota(jnp.int32, (tm, 1), 0)
    pos = jnp.where(pos >= t_pad, pos - t_pad, pos)
    seq = jnp.where(pos < t_real, seq_ref[...], 0.0)
    ext = jnp.concatenate([halo_ref[...], seq], axis=0)
    pos_ext = jnp.concatenate([jnp.zeros((HALO, 1), jnp.int32), pos], axis=0)
    pm, cv, cx = _pool_conv(ext, pos_ext, cw_ref[...])
    pm_ref[...] = pm.astype(BF16)
    cv_ref[...] = cv.astype(BF16)
    cx_ref[...] = cx
    halo_ref[...] = seq[tm - HALO:]


def _proj(h, gain, wg, ws, wq, wk, wv, qn, kn, seg, conv_w, t_pad, t_real):
    m = h.shape[0]
    assert TM <= t_pad and t_pad - t_real >= HALO
    row = lambda i: (i, 0)
    const = lambda i: (0, 0)
    widths = [(N_GATE, BF16), (D_SEQ, F32), (D_ATTN, BF16), (D_ATTN, F32), (D_ATTN, BF16), (D_ATTN, F32), (D_ATTN, BF16),
              (D_POOL, BF16), (D_CONV, BF16), (D_CONV, F32)]
    return pl.pallas_call(
        functools.partial(_proj_kernel, t_pad=t_pad, t_real=t_real),
        grid=(m // TM,),
        in_specs=[
            pl.BlockSpec((TM, D_MODEL), row),
            pl.BlockSpec((1, D_MODEL), const),
            _resident((D_MODEL, N_GATE)),
            _resident((D_MODEL, D_SEQ)),
            _resident((D_MODEL, D_ATTN)),
            _resident((D_MODEL, D_ATTN)),
            _resident((D_MODEL, D_ATTN)),
            pl.BlockSpec((1, D_ATTN), const),
            pl.BlockSpec((1, D_ATTN), const),
            _resident((D_ATTN, D_ATTN)),
            pl.BlockSpec((8, D_CONV), const),
        ],
        out_specs=[pl.BlockSpec((TM, w), row) for w, _ in widths],
        out_shape=[jax.ShapeDtypeStruct((m, w), dt) for w, dt in widths],
        scratch_shapes=[pltpu.VMEM((HALO, D_SEQ), F32)],
        compiler_params=_params(("arbitrary",)),
        name="proj",
    )(h, gain, wg, ws, wq, wk, wv, qn, kn, seg, conv_w)


def _scores(qs, kblk):
    rows = qs[0].shape[0]
    z = lax.dot_general(jnp.concatenate(qs, axis=0), kblk, (((1,), (1,)), ((), ())), preferred_element_type=F32)
    return [z[h * rows:(h + 1) * rows] for h in range(len(qs))]


def _softplus_bits(zs, valid):
    out = []
    for z in zs:
        sp = jnp.maximum(jnp.log(1.0 + jnp.exp2(jnp.minimum(z, SP_CLAMP))) * LOG2E, z)
        if valid is not None:
            sp = jnp.where(valid, sp, 0.0)
        out.append(sp.astype(BF16))
    return out


def _stick_weights(zs, sps, tri, carries, valid):
    cums = [jnp.dot(sp, tri, preferred_element_type=F32) for sp in sps]
    probs = []
    for z, cum, carry in zip(zs, cums, carries):
        a = jnp.exp2(z - cum - carry)
        if valid is not None:
            a = jnp.where(valid, a, 0.0)
        probs.append(a.astype(BF16))
    return probs, [carry + cum[:, 0:1] for carry, cum in zip(carries, cums)]


def _attn_prompt_kernel(q_ref, k_ref, v_ref, tri_ref, o_ref, acc_ref, car_ref, z_ref, p_ref):
    i = pl.program_id(2)
    blk = ATT_BLOCK
    nh = ATT_HEADS
    lane_head = lax.broadcasted_iota(jnp.int32, (blk, nh * HEAD_DIM), 1) // HEAD_DIM
    q2 = q_ref[...]
    qs = [jnp.where(lane_head == h, q2, jnp.zeros_like(q2)) for h in range(nh)]
    tri = tri_ref[...]

    def scores(j):
        kblk = k_ref[pl.ds(pl.multiple_of(j * blk, blk), blk), :]
        return _scores(qs, kblk)

    def weighted_values(j):
        vblk = v_ref[pl.ds(pl.multiple_of(j * blk, blk), blk), :]
        vstack = jnp.concatenate([jnp.where(lane_head == h, vblk, jnp.zeros_like(vblk)) for h in range(nh)], axis=0)
        return jnp.dot(p_ref[...], vstack, preferred_element_type=F32)

    def weights(zs, sps, valid):
        probs, cars = _stick_weights(zs, sps, tri, [car_ref[h] for h in range(nh)], valid)
        for h in range(nh):
            p_ref[:, h * blk:(h + 1) * blk] = probs[h]
            car_ref[h] = cars[h]

    def store_scores(zs):
        for h in range(nh):
            z_ref[h] = zs[h]

    car_ref[...] = jnp.zeros_like(car_ref)
    store_scores(scores(i))
    z_next = scores(jnp.maximum(i - 1, 0))
    row = lax.broadcasted_iota(jnp.int32, (blk, blk), 0)
    col = lax.broadcasted_iota(jnp.int32, (blk, blk), 1)
    zs = [z_ref[h] for h in range(nh)]
    weights(zs, _softplus_bits(zs, col < row), col < row)
    store_scores(z_next)
    acc_ref[...] = jnp.zeros_like(acc_ref)

    def settled():
        return jnp.min(car_ref[...]) >= CARRY_SETTLED

    def body(state):
        t, _ = state
        j = i - t
        zs = [z_ref[h] for h in range(nh)]
        sps = _softplus_bits(zs, None)
        acc_ref[...] += weighted_values(j + 1)
        z_next = scores(jnp.maximum(j - 1, 0))
        weights(zs, sps, None)
        store_scores(z_next)
        return t + 1, settled()

    t_end, _ = lax.while_loop(lambda state: (state[0] <= i) & jnp.logical_not(state[1]), body,
                              (jnp.int32(1), settled()))
    o_ref[...] = (acc_ref[...] + weighted_values(i - t_end + 1)).astype(o_ref.dtype)


def _attn_prompt(q, kb, vb, tri, n_batch, t_pad):
    nq = t_pad // ATT_BLOCK
    pair = ATT_HEADS * HEAD_DIM
    return pl.pallas_call(
        _attn_prompt_kernel,
        grid=(n_batch, N_HEADS // ATT_HEADS, nq),
        in_specs=[
            pl.BlockSpec((ATT_BLOCK, pair), lambda b, h, i: (b * nq + i, h)),
            pl.BlockSpec((t_pad, pair), lambda b, h, i: (b, h)),
            pl.BlockSpec((t_pad, pair), lambda b, h, i: (b, h)),
            pl.BlockSpec((ATT_BLOCK, ATT_BLOCK), lambda b, h, i: (0, 0)),
        ],
        out_specs=pl.BlockSpec((ATT_BLOCK, pair), lambda b, h, i: (b * nq + i, h)),
        out_shape=jax.ShapeDtypeStruct((n_batch * t_pad, D_ATTN), BF16),
        scratch_shapes=[pltpu.VMEM((ATT_BLOCK, pair), F32), pltpu.VMEM((ATT_HEADS, ATT_BLOCK, 1), F32),
                        pltpu.VMEM((ATT_HEADS, ATT_BLOCK, ATT_BLOCK), F32),
                        pltpu.VMEM((ATT_BLOCK, ATT_HEADS * ATT_BLOCK), BF16)],
        compiler_params=_params(("parallel", "parallel", "arbitrary")),
        name="attn_prompt",
    )(q, kb, vb, tri)


def _attn_sample_kernel(*refs, dec_seq, n_blocks, first):
    blk = ATT_BLOCK
    rows = N_HEADS * dec_seq
    if first:
        q_ref, kn_ref, vn_ref, kc_ref, vc_ref, tri_ref, o_ref, acc_ref, car_ref = refs
    else:
        q_ref, kc_ref, vc_ref, tri_ref, acc_in_ref, car_in_ref, o_ref, acc_ref, car_ref = refs
    q = q_ref[...]
    lane_head = lax.broadcasted_iota(jnp.int32, (dec_seq, D_ATTN), 1) // HEAD_DIM
    qs = jnp.concatenate([jnp.where(lane_head == h, q, jnp.zeros_like(q)) for h in range(N_HEADS)], axis=0)
    tri = tri_ref[...]

    if first:
        pad = jnp.zeros((blk - dec_seq, D_ATTN), BF16)
        k_new = jnp.concatenate([kn_ref[...], pad], axis=0)
        v_new = jnp.concatenate([vn_ref[...], pad], axis=0)
        tq = lax.broadcasted_iota(jnp.int32, (rows, blk), 0) % dec_seq
        col = lax.broadcasted_iota(jnp.int32, (rows, blk), 1)
        zs = _scores([qs], k_new)
        (a,), (car,) = _stick_weights(zs, _softplus_bits(zs, col < tq), tri, [jnp.zeros((rows, 1), F32)], col < tq)
        acc = jnp.dot(a, v_new, preferred_element_type=F32)
    else:
        acc, car = acc_in_ref[...], car_in_ref[...]
    for j in reversed(range(n_blocks)):
        kt = kc_ref[0, 0, :, j * blk:(j + 1) * blk].astype(BF16)
        vt = vc_ref[0, 0, :, j * blk:(j + 1) * blk].astype(BF16)
        zs = [jnp.dot(qs, kt, preferred_element_type=F32)]
        (a,), (car,) = _stick_weights(zs, _softplus_bits(zs, None), tri, [car], None)
        acc = acc + lax.dot_general(a, vt, (((1,), (1,)), ((), ())), preferred_element_type=F32)
    acc_ref[...] = acc
    car_ref[...] = car
    o = jnp.zeros((dec_seq, D_ATTN), F32)
    for h in range(N_HEADS):
        o = o + jnp.where(lane_head == h, acc[h * dec_seq:(h + 1) * dec_seq, :], 0.0)
    o_ref[...] = o.astype(o_ref.dtype)


def _attn_sample_call(q, new_kv, cache_kt, cache_vt, layer, key0, n, tri, state, row0, n_streams, dec_seq):
    assert n % ATT_BLOCK == 0 and key0 % n == 0 and row0 % dec_seq == 0 and dec_seq % 16 == 0
    blk0 = row0 // dec_seq
    rows = N_HEADS * dec_seq
    new_rows = pl.BlockSpec((dec_seq, D_ATTN), lambda b: (blk0 + b, 0))
    cache = pl.BlockSpec((1, 1, D_ATTN, n), lambda b: (layer, b, 0, key0 // n))
    tri_spec = pl.BlockSpec((ATT_BLOCK, ATT_BLOCK), lambda b: (0, 0))
    acc_spec = pl.BlockSpec((rows, D_ATTN), lambda b: (b, 0))
    car_spec = pl.BlockSpec((rows, 1), lambda b: (b, 0))
    if state is None:
        operands = (q,) + tuple(new_kv) + (cache_kt, cache_vt, tri)
        in_specs = [new_rows, new_rows, new_rows, cache, cache, tri_spec]
    else:
        operands = (q, cache_kt, cache_vt, tri) + tuple(state)
        in_specs = [new_rows, cache, cache, tri_spec, acc_spec, car_spec]
    return pl.pallas_call(
        functools.partial(_attn_sample_kernel, dec_seq=dec_seq, n_blocks=n // ATT_BLOCK, first=state is None),
        grid=(n_streams,),
        in_specs=in_specs,
        out_specs=[pl.BlockSpec((dec_seq, D_ATTN), lambda b: (b, 0)), acc_spec, car_spec],
        out_shape=[jax.ShapeDtypeStruct((n_streams * dec_seq, D_ATTN), BF16),
                   jax.ShapeDtypeStruct((n_streams * rows, D_ATTN), F32),
                   jax.ShapeDtypeStruct((n_streams * rows, 1), F32)],
        compiler_params=_params(("parallel",)),
        name="attn_sample",
    )(*operands)


def _attn_sample(q, kb, vb, cache_kt, cache_vt, layer, tri, row0, n_streams, dec_seq):
    past_len = cache_kt.shape[3]
    split = past_len - ATT_BLOCK
    o, acc, car = _attn_sample_call(q, (kb, vb), cache_kt, cache_vt, layer, split, ATT_BLOCK, tri, None,
                                    row0, n_streams, dec_seq)
    if split == 0:
        return o

    def earlier_keys(_):
        return _attn_sample_call(q, None, cache_kt, cache_vt, layer, 0, split, tri, (acc, car),
                                 row0, n_streams, dec_seq)[0]

    return lax.cond(jnp.min(car) >= CARRY_SETTLED, lambda _: o, earlier_keys, None)


def _seqmix_kernel(ext_ref, cw_ref, pm_ref, cv_ref, cx_ref, *, pos0):
    ext = ext_ref[...]
    ext = jnp.concatenate([jnp.zeros((HALO, D_SEQ), F32), ext], axis=0)
    pos = jnp.full((ext.shape[0], 1), pos0, jnp.int32)
    pm, cv, cx = _pool_conv(ext, pos, cw_ref[...])
    pm_ref[...] = pm.astype(BF16)
    cv_ref[...] = cv.astype(BF16)
    cx_ref[...] = cx


def _seqmix_sample(ext, conv_w, pos0):
    rows = ext.shape[0]
    full = lambda w: pl.BlockSpec((rows, w), lambda: (0, 0))
    out = lambda dt: jax.ShapeDtypeStruct((rows, D_POOL), dt)
    return pl.pallas_call(
        functools.partial(_seqmix_kernel, pos0=pos0),
        in_specs=[full(D_SEQ), pl.BlockSpec((8, D_CONV), lambda: (0, 0))],
        out_specs=[full(D_POOL)] * 3,
        out_shape=[out(BF16), out(BF16), out(F32)],
        compiler_params=pltpu.CompilerParams(vmem_limit_bytes=VMEM_LIMIT),
        name="seqmix",
    )(ext, conv_w)


def _mixout_kernel(h_ref, g_ref, pm_ref, cv_ref, op_ref, os_ref, wgrp_ref, ps_ref, pp_ref, cp_ref, ap_ref, wo_ref,
                   out_ref):
    o = jnp.where(pl.program_id(0) == pl.num_programs(0) - 1, os_ref[...], op_ref[...])
    ya = jnp.dot(pm_ref[...], wgrp_ref[...], preferred_element_type=F32) * ps_ref[...]
    ya = jnp.dot(ya.astype(BF16), pp_ref[...], preferred_element_type=F32)
    yb = jnp.dot(cv_ref[...], cp_ref[...], preferred_element_type=F32)
    yc = jnp.dot(o, ap_ref[...], preferred_element_type=F32)
    g = g_ref[...].astype(F32)
    mixed = g[:, 0:D_MODEL] * ya + g[:, D_MODEL:2 * D_MODEL] * yb + g[:, 2 * D_MODEL:3 * D_MODEL] * yc
    out_ref[...] = h_ref[...] + jnp.dot(mixed.astype(BF16), wo_ref[...], preferred_element_type=F32)


def _mixout(h, gates, pm, cv, o_prompt, o_sample, wgrp, pscale, pool_proj, conv_proj, attn_proj, w_out):
    m = h.shape[0]
    n_prompt_tiles = o_prompt.shape[0] // TM
    assert o_prompt.shape[0] % TM == 0 and o_sample.shape[0] == TM and m == o_prompt.shape[0] + TM
    row = lambda i: (i, 0)
    const = lambda i: (0, 0)
    return pl.pallas_call(
        _mixout_kernel,
        grid=(m // TM,),
        in_specs=[
            pl.BlockSpec((TM, D_MODEL), row),
            pl.BlockSpec((TM, N_GATE), row),
            pl.BlockSpec((TM, D_POOL), row),
            pl.BlockSpec((TM, D_CONV), row),
            pl.BlockSpec((TM, D_ATTN), lambda i: (jnp.minimum(i, n_prompt_tiles - 1), 0)),
            pl.BlockSpec((TM, D_ATTN), const),
            pl.BlockSpec((D_POOL, D_POOL), const),
            pl.BlockSpec((1, D_POOL), const),
            pl.BlockSpec((D_POOL, D_MODEL), const),
            pl.BlockSpec((D_CONV, D_MODEL), const),
            pl.BlockSpec((D_ATTN, D_MODEL), const),
            pl.BlockSpec((D_MODEL, D_MODEL), const),
        ],
        out_specs=pl.BlockSpec((TM, D_MODEL), row),
        out_shape=jax.ShapeDtypeStruct((m, D_MODEL), F32),
        compiler_params=_params(("parallel",)),
        name="mixout",
    )(h, gates, pm, cv, o_prompt, o_sample, wgrp, pscale, pool_proj, conv_proj, attn_proj, w_out)


def _block_diag(w):
    g, c, d = w.shape
    eye = jnp.eye(g, dtype=w.dtype)
    return (eye[:, None, :, None] * w[:, :, None, :]).reshape(g * c, g * d)


def kernel(x_prompt, x_sample, cache_k, cache_v, state_pool, state_conv, meta, ffn1_norm, ffn1_w_gate, ffn1_w_up, ffn1_w_down, mix_norm, w_in, pool_w, pool_scale, pool_proj, conv_w, conv_proj, q_norm, k_norm, attn_proj, w_out, ffn2_norm, ffn2_w_gate, ffn2_w_up, ffn2_w_down):
    n_batch, seq, _ = x_prompt.shape
    n_streams, dec_seq, _ = x_sample.shape
    depth = w_in.shape[0]
    past_len = cache_k.shape[2]
    t_real = N_META + seq
    t_pad = -(-t_real // ATT_BLOCK) * ATT_BLOCK
    rows_p = n_batch * t_pad
    rows_s = n_streams * dec_seq
    m_rows = rows_p + rows_s
    assert m_rows % TM == 0 and dec_seq > POOL_STATE and (HALO + dec_seq) % 8 == 0

    tail = jnp.zeros((t_pad - t_real, D_MODEL), F32)
    pieces = [piece for b in range(n_batch) for piece in (meta.astype(F32), x_prompt[b], tail)]
    h = jnp.concatenate(pieces + [x_sample.reshape(rows_s, D_MODEL)], axis=0)

    cache_kt = cache_k.transpose(0, 1, 3, 4, 2).reshape(depth, n_streams, D_ATTN, past_len)
    cache_vt = cache_v.transpose(0, 1, 3, 4, 2).reshape(depth, n_streams, D_ATTN, past_len)

    c_q = D_SEQ
    c_k, c_v, c_g = c_q + D_ATTN, c_q + 2 * D_ATTN, c_q + 3 * D_ATTN
    tri = (np.arange(ATT_BLOCK)[:, None] >= np.arange(ATT_BLOCK)[None, :])
    tri = jnp.asarray(tri, BF16)
    seg = jnp.asarray(np.kron(np.eye(N_HEADS), np.full((HEAD_DIM, HEAD_DIM), 1.0 / HEAD_DIM)), BF16)

    outs = [[] for _ in range(8)]
    for l in range(depth):
        h = _ffn(h, ffn1_norm[l][None], ffn1_w_gate[l].astype(BF16), ffn1_w_up[l].astype(BF16),
                 ffn1_w_down[l].astype(BF16))
        w = w_in[l].astype(BF16)
        cw = jnp.zeros((8, D_CONV), F32).at[:CONV_W].set(conv_w[l])
        gates, sq, q, k, kb, v, vb, pm, cv, cx_p = _proj(
            h, mix_norm[l][None], w[:, c_g:], w[:, :c_q], w[:, c_q:c_k], w[:, c_k:c_v], w[:, c_v:c_g],
            q_norm[l].reshape(1, D_ATTN), k_norm[l].reshape(1, D_ATTN), seg, cw, t_pad, t_real)

        o_prompt = _attn_prompt(q, kb, vb, tri, n_batch, t_pad)
        o_sample = _attn_sample(q, kb, vb, cache_kt, cache_vt, l, tri, rows_p, n_streams, dec_seq)

        seq_s = sq[rows_p:].reshape(n_streams, dec_seq, D_SEQ)
        hist = jnp.zeros((n_streams, HALO, D_SEQ), F32)
        hist = hist.at[:, HALO - POOL_STATE:, 0:D_POOL].set(state_pool[l])
        hist = hist.at[:, HALO - (CONV_W - 1):, D_POOL:D_POOL + D_CONV].set(state_conv[l])
        hist = hist.at[:, :, D_POOL + 2 * D_CONV:].set(1.0)
        ext_s = jnp.concatenate([hist, seq_s], axis=1).reshape(n_streams * (HALO + dec_seq), D_SEQ)
        pm_s, cv_s, cx_s = _seqmix_sample(ext_s, cw, past_len)
        data = lambda t: t.reshape(n_streams, HALO + dec_seq, -1)[:, HALO:]
        pm = lax.dynamic_update_slice(pm, data(pm_s).reshape(rows_s, D_POOL), (rows_p, 0))
        cv = lax.dynamic_update_slice(cv, data(cv_s).reshape(rows_s, D_CONV), (rows_p, 0))

        h = _mixout(h, gates, pm, cv, o_prompt, o_sample, _block_diag(pool_w[l]).astype(BF16), pool_scale[l][None],
                    pool_proj[l].astype(BF16), conv_proj[l].astype(BF16), attn_proj[l].astype(BF16),
                    w_out[l].astype(BF16))
        h = _ffn(h, ffn2_norm[l][None], ffn2_w_gate[l].astype(BF16), ffn2_w_up[l].astype(BF16),
                 ffn2_w_down[l].astype(BF16))

        prompt = lambda t, lo, hi: jnp.stack([t[b * t_pad + lo:b * t_pad + hi] for b in range(n_batch)])
        sample = lambda t: t[rows_p:].reshape(n_streams, dec_seq, -1)
        heads = lambda t: t.reshape(t.shape[0], t.shape[1], N_HEADS, HEAD_DIM)
        outs[0].append(heads(prompt(k, 0, t_real)))
        outs[1].append(heads(prompt(v, 0, t_real)))
        outs[2].append(prompt(sq, t_real - POOL_STATE, t_real)[:, :, :D_POOL])
        outs[3].append(prompt(cx_p, t_real - (CONV_W - 1), t_real))
        outs[4].append(heads(sample(k)))
        outs[5].append(heads(sample(v)))
        outs[6].append(sample(sq)[:, dec_seq - POOL_STATE:, :D_POOL])
        outs[7].append(data(cx_s)[:, dec_seq - (CONV_W - 1):])

    y_prompt = jnp.stack([h[b * t_pad + N_META:b * t_pad + t_real] for b in range(n_batch)])
    y_sample = h[rows_p:].reshape(n_streams, dec_seq, D_MODEL)
    return (y_prompt, y_sample) + tuple(jnp.stack(o) for o in outs)
```

```python
import functools

import jax
import jax.numpy as jnp
import numpy as np
from jax import lax
from jax.experimental import pallas as pl
from jax.experimental.pallas import tpu as pltpu

F32 = jnp.float32
BF16 = jnp.bfloat16

D_MODEL = 1024
N_META = 16
D_POOL = 256
POOL_WINDOWS = (2, 4, 8, 16)
POOL_GROUP = 64
POOL_STATE = 15
D_CONV = 256
CONV_W = 3
N_HEADS = 8
HEAD_DIM = 64
D_ATTN = N_HEADS * HEAD_DIM
D_FF = 2816
RMS_EPS = 1e-6
LOG2E = 1.4426950408889634
Q_SCALE = HEAD_DIM ** -0.5 * LOG2E
SP_CLAMP = 64.0
CARRY_SETTLED = 160.0
N_GATE = 3 * D_MODEL
D_SEQ = D_POOL + 3 * D_CONV
COL_SEQ, COL_Q, COL_K, COL_V, COL_GATE = 0, D_SEQ, D_SEQ + D_ATTN, D_SEQ + 2 * D_ATTN, D_SEQ + 3 * D_ATTN
IN_COLS = COL_GATE + N_GATE

ATT_BLOCK = 256
ATT_HEADS = 4
HALO = 16
TM = 512
PROJ_CHUNK = 512
VMEM_LIMIT = 56 * 1024 * 1024


def _params(sem):
    return pltpu.CompilerParams(dimension_semantics=sem, vmem_limit_bytes=VMEM_LIMIT)


def _rms_rows(x, gain):
    ms = jnp.mean(x * x, axis=-1, keepdims=True)
    return x * lax.rsqrt(ms + RMS_EPS) * gain


def _ffn_kernel(x_ref, g_ref, wg_ref, wu_ref, wd_ref, o_ref, *tail_ref):
    x = x_ref[...]
    xn = _rms_rows(x, g_ref[0]).astype(BF16)
    g = jnp.dot(xn, wg_ref[0], preferred_element_type=F32)
    u = jnp.dot(xn, wu_ref[0], preferred_element_type=F32)
    a = (g * jax.nn.sigmoid(g) * u).astype(BF16)
    y = x + 0.5 * jnp.dot(a, wd_ref[0], preferred_element_type=F32)
    if not tail_ref:
        o_ref[...] = y
    else:
        last = pl.num_programs(0) - 1

        @pl.when(pl.program_id(0) < last)
        def _():
            o_ref[...] = y

        @pl.when(pl.program_id(0) == last)
        def _():
            tail_ref[0][...] = y


def _resident(shape):
    return pl.BlockSpec(shape, lambda *_: (0,) * len(shape), pipeline_mode=pl.Buffered(1))


def _layer_resident(shape, layer):
    return pl.BlockSpec((1,) + shape, lambda *_: (layer,) + (0,) * len(shape), pipeline_mode=pl.Buffered(1))


def _ffn(x, gain, wg, wu, wd, layer, split_tail=False):
    m = x.shape[0]
    tiles = m // TM
    row = pl.BlockSpec((TM, D_MODEL), lambda i: (i, 0))
    if split_tail:
        out_specs = [pl.BlockSpec((TM, D_MODEL), lambda i: (jnp.minimum(i, tiles - 2), 0)),
                     pl.BlockSpec((TM, D_MODEL), lambda i: (0, 0))]
        out_shape = [jax.ShapeDtypeStruct((m - TM, D_MODEL), F32), jax.ShapeDtypeStruct((TM, D_MODEL), F32)]
    else:
        out_specs, out_shape = row, jax.ShapeDtypeStruct((m, D_MODEL), F32)
    return pl.pallas_call(
        _ffn_kernel,
        grid=(tiles,),
        in_specs=[
            row,
            _layer_resident((1, D_MODEL), layer),
            _layer_resident((D_MODEL, D_FF), layer),
            _layer_resident((D_MODEL, D_FF), layer),
            _layer_resident((D_FF, D_MODEL), layer),
        ],
        out_specs=out_specs,
        out_shape=out_shape,
        compiler_params=_params(("arbitrary",) if split_tail else ("parallel",)),
        name="ffn",
    )(x, gain, wg, wu, wd)


def _pool_conv(ext, pos, cw):
    a = ext[:, 0:D_POOL]
    cx = ext[:, D_POOL + 2 * D_CONV:D_POOL + 3 * D_CONV] * ext[:, D_POOL:D_POOL + D_CONV]
    sums = []
    s = a
    for w in POOL_WINDOWS:
        s = s + pltpu.roll(s, w // 2, 0)
        sums.append(s)
    lane_group = lax.broadcasted_iota(jnp.int32, a.shape, 1) // POOL_GROUP
    mean = jnp.zeros_like(a)
    for g, w in enumerate(POOL_WINDOWS):
        cnt = jnp.clip(pos + 1, 1, w).astype(F32)
        mean = jnp.where(lane_group == g, sums[g] / cnt, mean)
    y = pltpu.roll(cx, 2, 0) * cw[0:1] + pltpu.roll(cx, 1, 0) * cw[1:2] + cx * cw[2:3]
    gated = ext[:, D_POOL + D_CONV:D_POOL + 2 * D_CONV] * y
    return (mean - a)[HALO:], gated[HALO:], cx[HALO:]


def _head_rms(p, seg, gain):
    pp = p * p
    hi = pp.astype(BF16)
    lo = (pp - hi.astype(F32)).astype(BF16)
    ms = jnp.dot(hi, seg, preferred_element_type=F32) + jnp.dot(lo, seg, preferred_element_type=F32)
    return p * lax.rsqrt(ms + RMS_EPS) * gain


def _proj_kernel(h_ref, g_ref, w_ref, qn_ref, kn_ref, seg_ref, cw_ref,
                 gate_ref, seq_ref, q_ref, k_ref, kb_ref, v_ref, vb_ref, pm_ref, cv_ref, cx_ref, halo_ref,
                 *, t_pad, t_real):
    i = pl.program_id(0)

    @pl.when(i == 0)
    def _():
        halo_ref[...] = jnp.zeros_like(halo_ref)

    u = _rms_rows(h_ref[...], g_ref[0]).astype(BF16)

    def project(col, width):
        return jnp.dot(u, w_ref[0, :, col:col + width], preferred_element_type=F32)

    ch = PROJ_CHUNK
    for c in range(N_GATE // ch):
        gate_ref[:, c * ch:(c + 1) * ch] = jax.nn.sigmoid(project(COL_GATE + c * ch, ch)).astype(BF16)
    for c in range(D_SEQ // ch):
        seq_ref[:, c * ch:(c + 1) * ch] = project(COL_SEQ + c * ch, ch)
    q = _head_rms(project(COL_Q, D_ATTN), seg_ref[...], qn_ref[0])
    q_ref[...] = (q * Q_SCALE).astype(BF16)
    k = _head_rms(project(COL_K, D_ATTN), seg_ref[...], kn_ref[0])
    k_ref[...] = k
    kb_ref[...] = k.astype(BF16)
    v = project(COL_V, D_ATTN)
    v_ref[...] = v
    vb_ref[...] = v.astype(BF16)

    tm = seq_ref.shape[0]
    row0 = i * tm
    pos0 = row0 - (row0 // t_pad) * t_pad
    pos = pos0 + lax.broadcasted_iota(jnp.int32, (tm, 1), 0)
    pos = jnp.where(pos >= t_pad, pos - t_pad, pos)
    seq = jnp.where(pos < t_real, seq_ref[...], 0.0)
    ext = jnp.concatenate([halo_ref[...], seq], axis=0)
    pos_ext = jnp.concatenate([jnp.zeros((HALO, 1), jnp.int32), pos], axis=0)
    pm, cv, cx = _pool_conv(ext, pos_ext, cw_ref[0])
    pm_ref[...] = pm.astype(BF16)
    cv_ref[...] = cv.astype(BF16)
    cx_ref[...] = cx
    halo_ref[...] = seq[tm - HALO:]


def _proj(h, gain, w, qn, kn, seg, conv_w, layer, t_pad, t_real):
    m = h.shape[0]
    assert TM <= t_pad and t_pad - t_real >= HALO
    row = lambda i: (i, 0)
    widths = [(N_GATE, BF16), (D_SEQ, F32), (D_ATTN, BF16), (D_ATTN, F32), (D_ATTN, BF16), (D_ATTN, F32), (D_ATTN, BF16),
              (D_POOL, BF16), (D_CONV, BF16), (D_CONV, F32)]
    return pl.pallas_call(
        functools.partial(_proj_kernel, t_pad=t_pad, t_real=t_real),
        grid=(m // TM,),
        in_specs=[
            pl.BlockSpec((TM, D_MODEL), row),
            _layer_resident((1, D_MODEL), layer),
            _layer_resident((D_MODEL, IN_COLS), layer),
            _layer_resident((1, D_ATTN), layer),
            _layer_resident((1, D_ATTN), layer),
            _resident((D_ATTN, D_ATTN)),
            _layer_resident((8, D_CONV), layer),
        ],
        out_specs=[pl.BlockSpec((TM, w), row) for w, _ in widths],
        out_shape=[jax.ShapeDtypeStruct((m, w), dt) for w, dt in widths],
        scratch_shapes=[pltpu.VMEM((HALO, D_SEQ), F32)],
        compiler_params=_params(("arbitrary",)),
        name="proj",
    )(h, gain, w, qn, kn, seg, conv_w)


def _scores(qs, kblk):
    rows = qs[0].shape[0]
    z = lax.dot_general(jnp.concatenate(qs, axis=0), kblk, (((1,), (1,)), ((), ())), preferred_element_type=F32)
    return [z[h * rows:(h + 1) * rows] for h in range(len(qs))]


def _softplus_bits(zs, valid):
    out = []
    for z in zs:
        sp = jnp.maximum(jnp.log(1.0 + jnp.exp2(jnp.minimum(z, SP_CLAMP))) * LOG2E, z)
        if valid is not None:
            sp = jnp.where(valid, sp, 0.0)
        out.append(sp.astype(BF16))
    return out


def _stick_weights(zs, sps, tri, carries, valid):
    cums = [jnp.dot(sp, tri, preferred_element_type=F32) for sp in sps]
    probs = []
    for z, cum, carry in zip(zs, cums, carries):
        a = jnp.exp2(z - cum - carry)
        if valid is not None:
            a = jnp.where(valid, a, 0.0)
        probs.append(a.astype(BF16))
    return probs, [carry + cum[:, 0:1] for carry, cum in zip(carries, cums)]


def _attn_prompt_kernel(q_ref, k_ref, v_ref, tri_ref, o_ref, acc_ref, car_ref, z_ref, p_ref):
    i = pl.program_id(2)
    blk = ATT_BLOCK
    nh = ATT_HEADS
    lane_head = lax.broadcasted_iota(jnp.int32, (blk, nh * HEAD_DIM), 1) // HEAD_DIM
    q2 = q_ref[...]
    qs = [jnp.where(lane_head == h, q2, jnp.zeros_like(q2)) for h in range(nh)]
    tri = tri_ref[...]

    def scores(j):
        kblk = k_ref[pl.ds(pl.multiple_of(j * blk, blk), blk), :]
        return _scores(qs, kblk)

    def weighted_values(j):
        vblk = v_ref[pl.ds(pl.multiple_of(j * blk, blk), blk), :]
        vstack = jnp.concatenate([jnp.where(lane_head == h, vblk, jnp.zeros_like(vblk)) for h in range(nh)], axis=0)
        return jnp.dot(p_ref[...], vstack, preferred_element_type=F32)

    def weights(zs, sps, valid):
        probs, cars = _stick_weights(zs, sps, tri, [car_ref[h] for h in range(nh)], valid)
        for h in range(nh):
            p_ref[:, h * blk:(h + 1) * blk] = probs[h]
            car_ref[h] = cars[h]

    def store_scores(zs):
        for h in range(nh):
            z_ref[h] = zs[h]

    car_ref[...] = jnp.zeros_like(car_ref)
    store_scores(scores(i))
    z_next = scores(jnp.maximum(i - 1, 0))
    row = lax.broadcasted_iota(jnp.int32, (blk, blk), 0)
    col = lax.broadcasted_iota(jnp.int32, (blk, blk), 1)
    zs = [z_ref[h] for h in range(nh)]
    weights(zs, _softplus_bits(zs, col < row), col < row)
    store_scores(z_next)
    acc_ref[...] = jnp.zeros_like(acc_ref)

    def settled():
        return jnp.min(car_ref[...]) >= CARRY_SETTLED

    def body(state):
        t, _ = state
        j = i - t
        zs = [z_ref[h] for h in range(nh)]
        sps = _softplus_bits(zs, None)
        acc_ref[...] += weighted_values(j + 1)
        z_next = scores(jnp.maximum(j - 1, 0))
        weights(zs, sps, None)
        store_scores(z_next)
        return t + 1, settled()

    t_end, _ = lax.while_loop(lambda state: (state[0] <= i) & jnp.logical_not(state[1]), body,
                              (jnp.int32(1), settled()))
    o_ref[...] = (acc_ref[...] + weighted_values(i - t_end + 1)).astype(o_ref.dtype)


def _attn_prompt(q, kb, vb, tri, n_batch, t_pad):
    nq = t_pad // ATT_BLOCK
    pair = ATT_HEADS * HEAD_DIM
    return pl.pallas_call(
        _attn_prompt_kernel,
        grid=(n_batch, N_HEADS // ATT_HEADS, nq),
        in_specs=[
            pl.BlockSpec((ATT_BLOCK, pair), lambda b, h, i: (b * nq + i, h)),
            pl.BlockSpec((t_pad, pair), lambda b, h, i: (b, h)),
            pl.BlockSpec((t_pad, pair), lambda b, h, i: (b, h)),
            pl.BlockSpec((ATT_BLOCK, ATT_BLOCK), lambda b, h, i: (0, 0)),
        ],
        out_specs=pl.BlockSpec((ATT_BLOCK, pair), lambda b, h, i: (b * nq + i, h)),
        out_shape=jax.ShapeDtypeStruct((n_batch * t_pad, D_ATTN), BF16),
        scratch_shapes=[pltpu.VMEM((ATT_BLOCK, pair), F32), pltpu.VMEM((ATT_HEADS, ATT_BLOCK, 1), F32),
                        pltpu.VMEM((ATT_HEADS, ATT_BLOCK, ATT_BLOCK), F32),
                        pltpu.VMEM((ATT_BLOCK, ATT_HEADS * ATT_BLOCK), BF16)],
        compiler_params=_params(("parallel", "parallel", "arbitrary")),
        name="attn_prompt",
    )(q, kb, vb, tri)


def _attn_sample_kernel(*refs, dec_seq, n_blocks, first):
    blk = ATT_BLOCK
    rows = N_HEADS * dec_seq
    if first:
        q_ref, kn_ref, vn_ref, kc_ref, vc_ref, tri_ref, o_ref, acc_ref, car_ref = refs
    else:
        q_ref, kc_ref, vc_ref, tri_ref, acc_in_ref, car_in_ref, o_ref, acc_ref, car_ref = refs
    q = q_ref[...]
    lane_head = lax.broadcasted_iota(jnp.int32, (dec_seq, D_ATTN), 1) // HEAD_DIM
    qs = jnp.concatenate([jnp.where(lane_head == h, q, jnp.zeros_like(q)) for h in range(N_HEADS)], axis=0)
    tri = tri_ref[...]

    if first:
        pad = jnp.zeros((blk - dec_seq, D_ATTN), BF16)
        k_new = jnp.concatenate([kn_ref[...], pad], axis=0)
        v_new = jnp.concatenate([vn_ref[...], pad], axis=0)
        tq = lax.broadcasted_iota(jnp.int32, (rows, blk), 0) % dec_seq
        col = lax.broadcasted_iota(jnp.int32, (rows, blk), 1)
        zs = _scores([qs], k_new)
        (a,), (car,) = _stick_weights(zs, _softplus_bits(zs, col < tq), tri, [jnp.zeros((rows, 1), F32)], col < tq)
        acc = jnp.dot(a, v_new, preferred_element_type=F32)
    else:
        acc, car = acc_in_ref[...], car_in_ref[...]
    for j in reversed(range(n_blocks)):
        kt = kc_ref[0, 0, :, j * blk:(j + 1) * blk].astype(BF16)
        vt = vc_ref[0, 0, :, j * blk:(j + 1) * blk].astype(BF16)
        zs = [jnp.dot(qs, kt, preferred_element_type=F32)]
        (a,), (car,) = _stick_weights(zs, _softplus_bits(zs, None), tri, [car], None)
        acc = acc + lax.dot_general(a, vt, (((1,), (1,)), ((), ())), preferred_element_type=F32)
    acc_ref[...] = acc
    car_ref[...] = car
    o = jnp.zeros((dec_seq, D_ATTN), F32)
    for h in range(N_HEADS):
        o = o + jnp.where(lane_head == h, acc[h * dec_seq:(h + 1) * dec_seq, :], 0.0)
    o_ref[...] = o.astype(o_ref.dtype)


def _attn_sample_call(q, new_kv, cache_kt, cache_vt, layer, key0, n, tri, state, row0, n_streams, dec_seq):
    assert n % ATT_BLOCK == 0 and key0 % n == 0 and row0 % dec_seq == 0 and dec_seq % 16 == 0
    blk0 = row0 // dec_seq
    rows = N_HEADS * dec_seq
    new_rows = pl.BlockSpec((dec_seq, D_ATTN), lambda b: (blk0 + b, 0))
    cache = pl.BlockSpec((1, 1, D_ATTN, n), lambda b: (layer, b, 0, key0 // n))
    tri_spec = pl.BlockSpec((ATT_BLOCK, ATT_BLOCK), lambda b: (0, 0))
    acc_spec = pl.BlockSpec((rows, D_ATTN), lambda b: (b, 0))
    car_spec = pl.BlockSpec((rows, 1), lambda b: (b, 0))
    if state is None:
        operands = (q,) + tuple(new_kv) + (cache_kt, cache_vt, tri)
        in_specs = [new_rows, new_rows, new_rows, cache, cache, tri_spec]
    else:
        operands = (q, cache_kt, cache_vt, tri) + tuple(state)
        in_specs = [new_rows, cache, cache, tri_spec, acc_spec, car_spec]
    return pl.pallas_call(
        functools.partial(_attn_sample_kernel, dec_seq=dec_seq, n_blocks=n // ATT_BLOCK, first=state is None),
        grid=(n_streams,),
        in_specs=in_specs,
        out_specs=[pl.BlockSpec((dec_seq, D_ATTN), lambda b: (b, 0)), acc_spec, car_spec],
        out_shape=[jax.ShapeDtypeStruct((n_streams * dec_seq, D_ATTN), BF16),
                   jax.ShapeDtypeStruct((n_streams * rows, D_ATTN), F32),
                   jax.ShapeDtypeStruct((n_streams * rows, 1), F32)],
        compiler_params=_params(("parallel",)),
        name="attn_sample",
    )(*operands)


def _attn_sample(q, kb, vb, cache_kt, cache_vt, layer, tri, row0, n_streams, dec_seq):
    past_len = cache_kt.shape[3]
    split = past_len - ATT_BLOCK
    o, acc, car = _attn_sample_call(q, (kb, vb), cache_kt, cache_vt, layer, split, ATT_BLOCK, tri, None,
                                    row0, n_streams, dec_seq)
    if split == 0:
        return o

    def earlier_keys(_):
        return _attn_sample_call(q, None, cache_kt, cache_vt, layer, 0, split, tri, (acc, car),
                                 row0, n_streams, dec_seq)[0]

    return lax.cond(jnp.min(car) >= CARRY_SETTLED, lambda _: o, earlier_keys, None)


def _seqmix_kernel(ext_ref, cw_ref, pm_ref, cv_ref, cx_ref, *, pos0):
    ext = ext_ref[...]
    ext = jnp.concatenate([jnp.zeros((HALO, D_SEQ), F32), ext], axis=0)
    pos = jnp.full((ext.shape[0], 1), pos0, jnp.int32)
    pm, cv, cx = _pool_conv(ext, pos, cw_ref[0])
    pm_ref[...] = pm.astype(BF16)
    cv_ref[...] = cv.astype(BF16)
    cx_ref[...] = cx


def _seqmix_sample(ext, conv_w, layer, pos0):
    rows = ext.shape[0]
    full = lambda w: pl.BlockSpec((rows, w), lambda i: (0, 0))
    out = lambda dt: jax.ShapeDtypeStruct((rows, D_POOL), dt)
    return pl.pallas_call(
        functools.partial(_seqmix_kernel, pos0=pos0),
        grid=(1,),
        in_specs=[full(D_SEQ), pl.BlockSpec((1, 8, D_CONV), lambda i: (layer, 0, 0))],
        out_specs=[full(D_POOL)] * 3,
        out_shape=[out(BF16), out(BF16), out(F32)],
        compiler_params=_params(("arbitrary",)),
        name="seqmix",
    )(ext, conv_w)


def _mixout_kernel(h_ref, g_ref, pm_ref, cv_ref, o_ref, pms_ref, cvs_ref, os_ref, wgrp_ref, ps_ref, pp_ref, cp_ref,
                   ap_ref, wo_ref, out_ref):
    tail = pl.program_id(0) == pl.num_programs(0) - 1
    pm = jnp.where(tail, pms_ref[...], pm_ref[...])
    cv = jnp.where(tail, cvs_ref[...], cv_ref[...])
    o = jnp.where(tail, os_ref[...], o_ref[...])
    ya = jnp.dot(pm, wgrp_ref[0], preferred_element_type=F32) * ps_ref[0]
    ya = jnp.dot(ya.astype(BF16), pp_ref[0], preferred_element_type=F32)
    yb = jnp.dot(cv, cp_ref[0], preferred_element_type=F32)
    yc = jnp.dot(o, ap_ref[0], preferred_element_type=F32)
    g = g_ref[...].astype(F32)
    mixed = g[:, 0:D_MODEL] * ya + g[:, D_MODEL:2 * D_MODEL] * yb + g[:, 2 * D_MODEL:3 * D_MODEL] * yc
    out_ref[...] = h_ref[...] + jnp.dot(mixed.astype(BF16), wo_ref[0], preferred_element_type=F32)


def _mixout(h, gates, pm, cv, o_prompt, pm_sample, cv_sample, o_sample, wgrp, pscale, pool_proj, conv_proj, attn_proj,
            w_out, layer):
    m = h.shape[0]
    n_prompt_tiles = m // TM - 1
    assert o_prompt.shape[0] == m - TM and all(t.shape[0] == TM for t in (pm_sample, cv_sample, o_sample))
    row = lambda i: (i, 0)
    prompt_row = lambda i: (jnp.minimum(i, n_prompt_tiles - 1), 0)
    const = lambda i: (0, 0)
    return pl.pallas_call(
        _mixout_kernel,
        grid=(m // TM,),
        in_specs=[
            pl.BlockSpec((TM, D_MODEL), row),
            pl.BlockSpec((TM, N_GATE), row),
            pl.BlockSpec((TM, D_POOL), prompt_row),
            pl.BlockSpec((TM, D_CONV), prompt_row),
            pl.BlockSpec((TM, D_ATTN), prompt_row),
            pl.BlockSpec((TM, D_POOL), const),
            pl.BlockSpec((TM, D_CONV), const),
            pl.BlockSpec((TM, D_ATTN), const),
            _layer_resident((D_POOL, D_POOL), layer),
            _layer_resident((1, D_POOL), layer),
            _layer_resident((D_POOL, D_MODEL), layer),
            _layer_resident((D_CONV, D_MODEL), layer),
            _layer_resident((D_ATTN, D_MODEL), layer),
            _layer_resident((D_MODEL, D_MODEL), layer),
        ],
        out_specs=pl.BlockSpec((TM, D_MODEL), row),
        out_shape=jax.ShapeDtypeStruct((m, D_MODEL), F32),
        compiler_params=_params(("parallel",)),
        name="mixout",
    )(h, gates, pm, cv, o_prompt, pm_sample, cv_sample, o_sample, wgrp, pscale, pool_proj, conv_proj, attn_proj, w_out)


def _block_diag(w):
    g, c, d = w.shape
    eye = jnp.eye(g, dtype=w.dtype)
    return (eye[:, None, :, None] * w[:, :, None, :]).reshape(g * c, g * d)


def kernel(x_prompt, x_sample, cache_k, cache_v, state_pool, state_conv, meta, ffn1_norm, ffn1_w_gate, ffn1_w_up, ffn1_w_down, mix_norm, w_in, pool_w, pool_scale, pool_proj, conv_w, conv_proj, q_norm, k_norm, attn_proj, w_out, ffn2_norm, ffn2_w_gate, ffn2_w_up, ffn2_w_down):
    n_batch, seq, _ = x_prompt.shape
    n_streams, dec_seq, _ = x_sample.shape
    depth = w_in.shape[0]
    past_len = cache_k.shape[2]
    t_real = N_META + seq
    t_pad = -(-t_real // ATT_BLOCK) * ATT_BLOCK
    rows_p = n_batch * t_pad
    rows_s = n_streams * dec_seq
    m_rows = rows_p + rows_s
    assert m_rows % TM == 0 and rows_s == TM and dec_seq > POOL_STATE and (HALO + dec_seq) % 8 == 0

    tail = jnp.zeros((t_pad - t_real, D_MODEL), F32)
    pieces = [piece for b in range(n_batch) for piece in (meta.astype(F32), x_prompt[b], tail)]
    h = jnp.concatenate(pieces + [x_sample.reshape(rows_s, D_MODEL)], axis=0)

    cache_kt = cache_k.transpose(0, 1, 3, 4, 2).reshape(depth, n_streams, D_ATTN, past_len)
    cache_vt = cache_v.transpose(0, 1, 3, 4, 2).reshape(depth, n_streams, D_ATTN, past_len)

    tri = (np.arange(ATT_BLOCK)[:, None] >= np.arange(ATT_BLOCK)[None, :])
    tri = jnp.asarray(tri, BF16)
    seg = jnp.asarray(np.kron(np.eye(N_HEADS), np.full((HEAD_DIM, HEAD_DIM), 1.0 / HEAD_DIM)), BF16)

    bf = lambda t: t.astype(BF16)
    rowvec = lambda t: t.reshape(depth, 1, -1)
    ffn1 = (rowvec(ffn1_norm), bf(ffn1_w_gate), bf(ffn1_w_up), bf(ffn1_w_down))
    ffn2 = (rowvec(ffn2_norm), bf(ffn2_w_gate), bf(ffn2_w_up), bf(ffn2_w_down))
    w_in_b = bf(w_in)
    mix_gain, q_gain, k_gain = rowvec(mix_norm), rowvec(q_norm), rowvec(k_norm)
    cw = jnp.zeros((depth, 8, D_CONV), F32).at[:, :CONV_W].set(conv_w)
    mix_w = (bf(jnp.stack([_block_diag(pool_w[l]) for l in range(depth)])), rowvec(pool_scale), bf(pool_proj),
             bf(conv_proj), bf(attn_proj), bf(w_out))

    outs = [[] for _ in range(8)]
    for l in range(depth):
        h = _ffn(h, *ffn1, l)
        gates, sq, q, k, kb, v, vb, pm, cv, cx_p = _proj(h, mix_gain, w_in_b, q_gain, k_gain, seg, cw, l, t_pad, t_real)

        o_prompt = _attn_prompt(q, kb, vb, tri, n_batch, t_pad)
        o_sample = _attn_sample(q, kb, vb, cache_kt, cache_vt, l, tri, rows_p, n_streams, dec_seq)

        seq_s = sq[rows_p:].reshape(n_streams, dec_seq, D_SEQ)
        hist = jnp.zeros((n_streams, HALO, D_SEQ), F32)
        hist = hist.at[:, HALO - POOL_STATE:, 0:D_POOL].set(state_pool[l])
        hist = hist.at[:, HALO - (CONV_W - 1):, D_POOL:D_POOL + D_CONV].set(state_conv[l])
        hist = hist.at[:, :, D_POOL + 2 * D_CONV:].set(1.0)
        ext_s = jnp.concatenate([hist, seq_s], axis=1).reshape(n_streams * (HALO + dec_seq), D_SEQ)
        pm_s, cv_s, cx_s = _seqmix_sample(ext_s, cw, l, past_len)
        data = lambda t: t.reshape(n_streams, HALO + dec_seq, -1)[:, HALO:]

        h = _mixout(h, gates, pm, cv, o_prompt, data(pm_s).reshape(rows_s, D_POOL), data(cv_s).reshape(rows_s, D_CONV),
                    o_sample, *mix_w, l)
        if l < depth - 1:
            h = _ffn(h, *ffn2, l)
        else:
            h, y_sample = _ffn(h, *ffn2, l, split_tail=True)

        prompt = lambda t, lo, hi: jnp.stack([t[b * t_pad + lo:b * t_pad + hi] for b in range(n_batch)])
        sample = lambda t: t[rows_p:].reshape(n_streams, dec_seq, -1)
        heads = lambda t: t.reshape(t.shape[0], t.shape[1], N_HEADS, HEAD_DIM)
        outs[0].append(heads(prompt(k, 0, t_real)))
        outs[1].append(heads(prompt(v, 0, t_real)))
        outs[2].append(prompt(sq, t_real - POOL_STATE, t_real)[:, :, :D_POOL])
        outs[3].append(prompt(cx_p, t_real - (CONV_W - 1), t_real))
        outs[4].append(heads(sample(k)))
        outs[5].append(heads(sample(v)))
        outs[6].append(sample(sq)[:, dec_seq - POOL_STATE:, :D_POOL])
        outs[7].append(data(cx_s)[:, dec_seq - (CONV_W - 1):])

    y_prompt = h.reshape(n_batch, t_pad, D_MODEL)[:, N_META:t_real]
    y_sample = y_sample.reshape(n_streams, dec_seq, D_MODEL)
    return (y_prompt, y_sample) + tuple(jnp.stack(o) for o in outs)
```

```python
import functools

import jax
import jax.numpy as jnp
import numpy as np
from jax import lax
from jax.experimental import pallas as pl
from jax.experimental.pallas import tpu as pltpu

F32 = jnp.float32
BF16 = jnp.bfloat16

D_MODEL = 1024
N_META = 16
D_POOL = 256
POOL_WINDOWS = (2, 4, 8, 16)
POOL_GROUP = 64
POOL_STATE = 15
D_CONV = 256
CONV_W = 3
N_HEADS = 8
HEAD_DIM = 64
D_ATTN = N_HEADS * HEAD_DIM
D_FF = 2816
RMS_EPS = 1e-6
LOG2E = 1.4426950408889634
Q_SCALE = HEAD_DIM ** -0.5 * LOG2E
SP_CLAMP = 64.0
CARRY_SETTLED = 160.0
N_GATE = 3 * D_MODEL
D_SEQ = D_POOL + 3 * D_CONV
COL_SEQ, COL_Q, COL_K, COL_V, COL_GATE = 0, D_SEQ, D_SEQ + D_ATTN, D_SEQ + 2 * D_ATTN, D_SEQ + 3 * D_ATTN
IN_COLS = COL_GATE + N_GATE

ATT_BLOCK = 256
ATT_HEADS = 4
HALO = 16
TM = 512
PROJ_CHUNK = 512
VMEM_LIMIT = 56 * 1024 * 1024


def _params(sem):
    return pltpu.CompilerParams(dimension_semantics=sem, vmem_limit_bytes=VMEM_LIMIT)


def _rms_rows(x, gain):
    ms = jnp.mean(x * x, axis=-1, keepdims=True)
    return x * lax.rsqrt(ms + RMS_EPS) * gain


def _ffn_kernel(x_ref, g_ref, wg_ref, wu_ref, wd_ref, o_ref, *tail_ref):
    x = x_ref[...]
    xn = _rms_rows(x, g_ref[0]).astype(BF16)
    g = jnp.dot(xn, wg_ref[0], preferred_element_type=F32)
    u = jnp.dot(xn, wu_ref[0], preferred_element_type=F32)
    a = (g * jax.nn.sigmoid(g) * u).astype(BF16)
    y = x + 0.5 * jnp.dot(a, wd_ref[0], preferred_element_type=F32)
    if not tail_ref:
        o_ref[...] = y
    else:
        last = pl.num_programs(0) - 1

        @pl.when(pl.program_id(0) < last)
        def _():
            o_ref[...] = y

        @pl.when(pl.program_id(0) == last)
        def _():
            tail_ref[0][...] = y


def _resident(shape):
    return pl.BlockSpec(shape, lambda *_: (0,) * len(shape), pipeline_mode=pl.Buffered(1))


def _layer_resident(shape, layer):
    return pl.BlockSpec((1,) + shape, lambda *_: (layer,) + (0,) * len(shape), pipeline_mode=pl.Buffered(1))


def _ffn(x, gain, wg, wu, wd, layer, split_tail=False):
    m = x.shape[0]
    tiles = m // TM
    row = pl.BlockSpec((TM, D_MODEL), lambda i: (i, 0))
    if split_tail:
        out_specs = [pl.BlockSpec((TM, D_MODEL), lambda i: (jnp.minimum(i, tiles - 2), 0)),
                     pl.BlockSpec((TM, D_MODEL), lambda i: (0, 0))]
        out_shape = [jax.ShapeDtypeStruct((m - TM, D_MODEL), F32), jax.ShapeDtypeStruct((TM, D_MODEL), F32)]
    else:
        out_specs, out_shape = row, jax.ShapeDtypeStruct((m, D_MODEL), F32)
    return pl.pallas_call(
        _ffn_kernel,
        grid=(tiles,),
        in_specs=[
            row,
            _layer_resident((1, D_MODEL), layer),
            _layer_resident((D_MODEL, D_FF), layer),
            _layer_resident((D_MODEL, D_FF), layer),
            _layer_resident((D_FF, D_MODEL), layer),
        ],
        out_specs=out_specs,
        out_shape=out_shape,
        compiler_params=_params(("arbitrary",) if split_tail else ("parallel",)),
        name="ffn",
    )(x, gain, wg, wu, wd)


def _pool_conv(ext, pos, cw):
    a = ext[:, 0:D_POOL]
    cx = ext[:, D_POOL + 2 * D_CONV:D_POOL + 3 * D_CONV] * ext[:, D_POOL:D_POOL + D_CONV]
    sums = []
    s = a
    for w in POOL_WINDOWS:
        s = s + pltpu.roll(s, w // 2, 0)
        sums.append(s)
    lane_group = lax.broadcasted_iota(jnp.int32, a.shape, 1) // POOL_GROUP
    mean = jnp.zeros_like(a)
    for g, w in enumerate(POOL_WINDOWS):
        cnt = jnp.clip(pos + 1, 1, w).astype(F32)
        mean = jnp.where(lane_group == g, sums[g] / cnt, mean)
    y = pltpu.roll(cx, 2, 0) * cw[0:1] + pltpu.roll(cx, 1, 0) * cw[1:2] + cx * cw[2:3]
    gated = ext[:, D_POOL + D_CONV:D_POOL + 2 * D_CONV] * y
    return (mean - a)[HALO:], gated[HALO:], cx[HALO:]


def _head_rms(p, seg, gain):
    ms = jnp.dot((p * p).astype(BF16), seg, preferred_element_type=F32)
    return p * lax.rsqrt(ms + RMS_EPS) * gain


def _proj_kernel(h_ref, g_ref, w_ref, qn_ref, kn_ref, seg_ref, cw_ref,
                 gate_ref, seq_ref, q_ref, k_ref, kb_ref, v_ref, vb_ref, pm_ref, cv_ref, cx_ref, halo_ref,
                 *, t_pad, t_real):
    i = pl.program_id(0)

    @pl.when(i == 0)
    def _():
        halo_ref[...] = jnp.zeros_like(halo_ref)

    u = _rms_rows(h_ref[...], g_ref[0]).astype(BF16)

    def project(col, width):
        return jnp.dot(u, w_ref[0, :, col:col + width], preferred_element_type=F32)

    ch = PROJ_CHUNK
    for c in range(N_GATE // ch):
        gate_ref[:, c * ch:(c + 1) * ch] = jax.nn.sigmoid(project(COL_GATE + c * ch, ch)).astype(BF16)
    for c in range(D_SEQ // ch):
        seq_ref[:, c * ch:(c + 1) * ch] = project(COL_SEQ + c * ch, ch)
    q = _head_rms(project(COL_Q, D_ATTN), seg_ref[...], qn_ref[0])
    q_ref[...] = (q * Q_SCALE).astype(BF16)
    k = _head_rms(project(COL_K, D_ATTN), seg_ref[...], kn_ref[0])
    k_ref[...] = k
    kb_ref[...] = k.astype(BF16)
    v = project(COL_V, D_ATTN)
    v_ref[...] = v
    vb_ref[...] = v.astype(BF16)

    tm = seq_ref.shape[0]
    row0 = i * tm
    pos0 = row0 - (row0 // t_pad) * t_pad
    pos = pos0 + lax.broadcasted_iota(jnp.int32, (tm, 1), 0)
    pos = jnp.where(pos >= t_pad, pos - t_pad, pos)
    seq = jnp.where(pos < t_real, seq_ref[...], 0.0)
    ext = jnp.concatenate([halo_ref[...], seq], axis=0)
    pos_ext = jnp.concatenate([jnp.zeros((HALO, 1), jnp.int32), pos], axis=0)
    pm, cv, cx = _pool_conv(ext, pos_ext, cw_ref[0])
    pm_ref[...] = pm.astype(BF16)
    cv_ref[...] = cv.astype(BF16)
    cx_ref[...] = cx
    halo_ref[...] = seq[tm - HALO:]


def _proj(h, gain, w, qn, kn, seg, conv_w, layer, t_pad, t_real):
    m = h.shape[0]
    assert TM <= t_pad and t_pad - t_real >= HALO
    row = lambda i: (i, 0)
    widths = [(N_GATE, BF16), (D_SEQ, F32), (D_ATTN, BF16), (D_ATTN, F32), (D_ATTN, BF16), (D_ATTN, F32), (D_ATTN, BF16),
              (D_POOL, BF16), (D_CONV, BF16), (D_CONV, F32)]
    return pl.pallas_call(
        functools.partial(_proj_kernel, t_pad=t_pad, t_real=t_real),
        grid=(m // TM,),
        in_specs=[
            pl.BlockSpec((TM, D_MODEL), row),
            _layer_resident((1, D_MODEL), layer),
            _layer_resident((D_MODEL, IN_COLS), layer),
            _layer_resident((1, D_ATTN), layer),
            _layer_resident((1, D_ATTN), layer),
            _resident((D_ATTN, D_ATTN)),
            _layer_resident((8, D_CONV), layer),
        ],
        out_specs=[pl.BlockSpec((TM, w), row) for w, _ in widths],
        out_shape=[jax.ShapeDtypeStruct((m, w), dt) for w, dt in widths],
        scratch_shapes=[pltpu.VMEM((HALO, D_SEQ), F32)],
        compiler_params=_params(("arbitrary",)),
        name="proj",
    )(h, gain, w, qn, kn, seg, conv_w)


def _scores(qs, kblk):
    rows = qs[0].shape[0]
    z = lax.dot_general(jnp.concatenate(qs, axis=0), kblk, (((1,), (1,)), ((), ())), preferred_element_type=F32)
    return [z[h * rows:(h + 1) * rows] for h in range(len(qs))]


def _softplus_bits(zs, valid):
    out = []
    for z in zs:
        sp = jnp.maximum(jnp.log(1.0 + jnp.exp2(jnp.minimum(z, SP_CLAMP))) * LOG2E, z)
        if valid is not None:
            sp = jnp.where(valid, sp, 0.0)
        out.append(sp.astype(BF16))
    return out


def _stick_weights(zs, sps, tri, carries, valid):
    cums = [jnp.dot(sp, tri, preferred_element_type=F32) for sp in sps]
    probs = []
    for z, cum, carry in zip(zs, cums, carries):
        a = jnp.exp2(z - cum - carry)
        if valid is not None:
            a = jnp.where(valid, a, 0.0)
        probs.append(a.astype(BF16))
    return probs, [carry + cum[:, 0:1] for carry, cum in zip(carries, cums)]


def _attn_prompt_kernel(q_ref, k_ref, v_ref, tri_ref, o_ref, acc_ref, car_ref, z_ref, p_ref):
    i = pl.program_id(2)
    blk = ATT_BLOCK
    nh = ATT_HEADS
    lane_head = lax.broadcasted_iota(jnp.int32, (blk, nh * HEAD_DIM), 1) // HEAD_DIM
    q2 = q_ref[...]
    qs = [jnp.where(lane_head == h, q2, jnp.zeros_like(q2)) for h in range(nh)]
    tri = tri_ref[...]

    def scores(j):
        kblk = k_ref[pl.ds(pl.multiple_of(j * blk, blk), blk), :]
        return _scores(qs, kblk)

    def weighted_values(j):
        vblk = v_ref[pl.ds(pl.multiple_of(j * blk, blk), blk), :]
        vstack = jnp.concatenate([jnp.where(lane_head == h, vblk, jnp.zeros_like(vblk)) for h in range(nh)], axis=0)
        return jnp.dot(p_ref[...], vstack, preferred_element_type=F32)

    def weights(zs, sps, valid):
        probs, cars = _stick_weights(zs, sps, tri, [car_ref[h] for h in range(nh)], valid)
        for h in range(nh):
            p_ref[:, h * blk:(h + 1) * blk] = probs[h]
            car_ref[h] = cars[h]
        return jnp.min(functools.reduce(jnp.minimum, cars)) >= CARRY_SETTLED

    def store_scores(zs):
        for h in range(nh):
            z_ref[h] = zs[h]

    car_ref[...] = jnp.zeros_like(car_ref)
    store_scores(scores(i))
    z_next = scores(jnp.maximum(i - 1, 0))
    row = lax.broadcasted_iota(jnp.int32, (blk, blk), 0)
    col = lax.broadcasted_iota(jnp.int32, (blk, blk), 1)
    zs = [z_ref[h] for h in range(nh)]
    settled = weights(zs, _softplus_bits(zs, col < row), col < row)
    store_scores(z_next)
    acc_ref[...] = jnp.zeros_like(acc_ref)

    def body(state):
        t, _ = state
        j = i - t
        zs = [z_ref[h] for h in range(nh)]
        sps = _softplus_bits(zs, None)
        acc_ref[...] += weighted_values(j + 1)
        z_next = scores(jnp.maximum(j - 1, 0))
        settled = weights(zs, sps, None)
        store_scores(z_next)
        return t + 1, settled

    t_end, _ = lax.while_loop(lambda state: (state[0] <= i) & jnp.logical_not(state[1]), body,
                              (jnp.int32(1), settled))
    o_ref[...] = (acc_ref[...] + weighted_values(i - t_end + 1)).astype(o_ref.dtype)


def _attn_prompt(q, kb, vb, tri, n_batch, t_pad):
    nq = t_pad // ATT_BLOCK
    pair = ATT_HEADS * HEAD_DIM
    return pl.pallas_call(
        _attn_prompt_kernel,
        grid=(n_batch, N_HEADS // ATT_HEADS, nq),
        in_specs=[
            pl.BlockSpec((ATT_BLOCK, pair), lambda b, h, i: (b * nq + i, h)),
            pl.BlockSpec((t_pad, pair), lambda b, h, i: (b, h)),
            pl.BlockSpec((t_pad, pair), lambda b, h, i: (b, h)),
            pl.BlockSpec((ATT_BLOCK, ATT_BLOCK), lambda b, h, i: (0, 0)),
        ],
        out_specs=pl.BlockSpec((ATT_BLOCK, pair), lambda b, h, i: (b * nq + i, h)),
        out_shape=jax.ShapeDtypeStruct((n_batch * t_pad, D_ATTN), BF16),
        scratch_shapes=[pltpu.VMEM((ATT_BLOCK, pair), F32), pltpu.VMEM((ATT_HEADS, ATT_BLOCK, 1), F32),
                        pltpu.VMEM((ATT_HEADS, ATT_BLOCK, ATT_BLOCK), F32),
                        pltpu.VMEM((ATT_BLOCK, ATT_HEADS * ATT_BLOCK), BF16)],
        compiler_params=_params(("parallel", "parallel", "arbitrary")),
        name="attn_prompt",
    )(q, kb, vb, tri)


def _attn_sample_kernel(*refs, dec_seq, n_blocks, first):
    blk = ATT_BLOCK
    rows = N_HEADS * dec_seq
    if first:
        q_ref, kn_ref, vn_ref, kc_ref, vc_ref, tri_ref, o_ref, acc_ref, car_ref = refs
    else:
        q_ref, kc_ref, vc_ref, tri_ref, acc_in_ref, car_in_ref, o_ref, acc_ref, car_ref = refs
    q = q_ref[...]
    lane_head = lax.broadcasted_iota(jnp.int32, (dec_seq, D_ATTN), 1) // HEAD_DIM
    qs = jnp.concatenate([jnp.where(lane_head == h, q, jnp.zeros_like(q)) for h in range(N_HEADS)], axis=0)
    tri = tri_ref[...]

    if first:
        pad = jnp.zeros((blk - dec_seq, D_ATTN), BF16)
        k_new = jnp.concatenate([kn_ref[...], pad], axis=0)
        v_new = jnp.concatenate([vn_ref[...], pad], axis=0)
        tq = lax.broadcasted_iota(jnp.int32, (rows, blk), 0) % dec_seq
        col = lax.broadcasted_iota(jnp.int32, (rows, blk), 1)
        zs = _scores([qs], k_new)
        (a,), (car,) = _stick_weights(zs, _softplus_bits(zs, col < tq), tri, [jnp.zeros((rows, 1), F32)], col < tq)
        acc = jnp.dot(a, v_new, preferred_element_type=F32)
    else:
        acc, car = acc_in_ref[...], car_in_ref[...]
    for j in reversed(range(n_blocks)):
        kt = kc_ref[0, 0, :, j * blk:(j + 1) * blk].astype(BF16)
        vt = vc_ref[0, 0, :, j * blk:(j + 1) * blk].astype(BF16)
        zs = [jnp.dot(qs, kt, preferred_element_type=F32)]
        (a,), (car,) = _stick_weights(zs, _softplus_bits(zs, None), tri, [car], None)
        acc = acc + lax.dot_general(a, vt, (((1,), (1,)), ((), ())), preferred_element_type=F32)
    acc_ref[...] = acc
    car_ref[...] = car
    o = jnp.zeros((dec_seq, D_ATTN), F32)
    for h in range(N_HEADS):
        o = o + jnp.where(lane_head == h, acc[h * dec_seq:(h + 1) * dec_seq, :], 0.0)
    o_ref[...] = o.astype(o_ref.dtype)


def _attn_sample_call(q, new_kv, cache_kt, cache_vt, layer, key0, n, tri, state, row0, n_streams, dec_seq):
    assert n % ATT_BLOCK == 0 and key0 % n == 0 and row0 % dec_seq == 0 and dec_seq % 16 == 0
    blk0 = row0 // dec_seq
    rows = N_HEADS * dec_seq
    new_rows = pl.BlockSpec((dec_seq, D_ATTN), lambda b: (blk0 + b, 0))
    cache = pl.BlockSpec((1, 1, D_ATTN, n), lambda b: (layer, b, 0, key0 // n))
    tri_spec = pl.BlockSpec((ATT_BLOCK, ATT_BLOCK), lambda b: (0, 0))
    acc_spec = pl.BlockSpec((rows, D_ATTN), lambda b: (b, 0))
    car_spec = pl.BlockSpec((rows, 1), lambda b: (b, 0))
    if state is None:
        operands = (q,) + tuple(new_kv) + (cache_kt, cache_vt, tri)
        in_specs = [new_rows, new_rows, new_rows, cache, cache, tri_spec]
    else:
        operands = (q, cache_kt, cache_vt, tri) + tuple(state)
        in_specs = [new_rows, cache, cache, tri_spec, acc_spec, car_spec]
    return pl.pallas_call(
        functools.partial(_attn_sample_kernel, dec_seq=dec_seq, n_blocks=n // ATT_BLOCK, first=state is None),
        grid=(n_streams,),
        in_specs=in_specs,
        out_specs=[pl.BlockSpec((dec_seq, D_ATTN), lambda b: (b, 0)), acc_spec, car_spec],
        out_shape=[jax.ShapeDtypeStruct((n_streams * dec_seq, D_ATTN), BF16),
                   jax.ShapeDtypeStruct((n_streams * rows, D_ATTN), F32),
                   jax.ShapeDtypeStruct((n_streams * rows, 1), F32)],
        compiler_params=_params(("parallel",)),
        name="attn_sample",
    )(*operands)


def _attn_sample(q, kb, vb, cache_kt, cache_vt, layer, tri, row0, n_streams, dec_seq):
    past_len = cache_kt.shape[3]
    split = past_len - ATT_BLOCK
    o, acc, car = _attn_sample_call(q, (kb, vb), cache_kt, cache_vt, layer, split, ATT_BLOCK, tri, None,
                                    row0, n_streams, dec_seq)
    if split == 0:
        return o

    def earlier_keys(_):
        return _attn_sample_call(q, None, cache_kt, cache_vt, layer, 0, split, tri, (acc, car),
                                 row0, n_streams, dec_seq)[0]

    return lax.cond(jnp.min(car) >= CARRY_SETTLED, lambda _: o, earlier_keys, None)


def _seqmix_kernel(ext_ref, cw_ref, pm_ref, cv_ref, cx_ref, *, pos0):
    ext = ext_ref[...]
    ext = jnp.concatenate([jnp.zeros((HALO, D_SEQ), F32), ext], axis=0)
    pos = jnp.full((ext.shape[0], 1), pos0, jnp.int32)
    pm, cv, cx = _pool_conv(ext, pos, cw_ref[0])
    pm_ref[...] = pm.astype(BF16)
    cv_ref[...] = cv.astype(BF16)
    cx_ref[...] = cx


def _seqmix_sample(ext, conv_w, layer, pos0):
    rows = ext.shape[0]
    full = lambda w: pl.BlockSpec((rows, w), lambda i: (0, 0))
    out = lambda dt: jax.ShapeDtypeStruct((rows, D_POOL), dt)
    return pl.pallas_call(
        functools.partial(_seqmix_kernel, pos0=pos0),
        grid=(1,),
        in_specs=[full(D_SEQ), pl.BlockSpec((1, 8, D_CONV), lambda i: (layer, 0, 0))],
        out_specs=[full(D_POOL)] * 3,
        out_shape=[out(BF16), out(BF16), out(F32)],
        compiler_params=_params(("arbitrary",)),
        name="seqmix",
    )(ext, conv_w)


def _mixout_kernel(h_ref, g_ref, pm_ref, cv_ref, o_ref, pms_ref, cvs_ref, os_ref, wgrp_ref, ps_ref, pp_ref, cp_ref,
                   ap_ref, wo_ref, out_ref):
    tail = pl.program_id(0) == pl.num_programs(0) - 1
    pm = jnp.where(tail, pms_ref[...], pm_ref[...])
    cv = jnp.where(tail, cvs_ref[...], cv_ref[...])
    o = jnp.where(tail, os_ref[...], o_ref[...])
    ya = jnp.dot(pm, wgrp_ref[0], preferred_element_type=F32) * ps_ref[0]
    ya = jnp.dot(ya.astype(BF16), pp_ref[0], preferred_element_type=F32)
    yb = jnp.dot(cv, cp_ref[0], preferred_element_type=F32)
    yc = jnp.dot(o, ap_ref[0], preferred_element_type=F32)
    g = g_ref[...].astype(F32)
    mixed = g[:, 0:D_MODEL] * ya + g[:, D_MODEL:2 * D_MODEL] * yb + g[:, 2 * D_MODEL:3 * D_MODEL] * yc
    out_ref[...] = h_ref[...] + jnp.dot(mixed.astype(BF16), wo_ref[0], preferred_element_type=F32)


def _mixout(h, gates, pm, cv, o_prompt, pm_sample, cv_sample, o_sample, wgrp, pscale, pool_proj, conv_proj, attn_proj,
            w_out, layer):
    m = h.shape[0]
    n_prompt_tiles = m // TM - 1
    assert o_prompt.shape[0] == m - TM and all(t.shape[0] == TM for t in (pm_sample, cv_sample, o_sample))
    row = lambda i: (i, 0)
    prompt_row = lambda i: (jnp.minimum(i, n_prompt_tiles - 1), 0)
    const = lambda i: (0, 0)
    return pl.pallas_call(
        _mixout_kernel,
        grid=(m // TM,),
        in_specs=[
            pl.BlockSpec((TM, D_MODEL), row),
            pl.BlockSpec((TM, N_GATE), row),
            pl.BlockSpec((TM, D_POOL), prompt_row),
            pl.BlockSpec((TM, D_CONV), prompt_row),
            pl.BlockSpec((TM, D_ATTN), prompt_row),
            pl.BlockSpec((TM, D_POOL), const),
            pl.BlockSpec((TM, D_CONV), const),
            pl.BlockSpec((TM, D_ATTN), const),
            _layer_resident((D_POOL, D_POOL), layer),
            _layer_resident((1, D_POOL), layer),
            _layer_resident((D_POOL, D_MODEL), layer),
            _layer_resident((D_CONV, D_MODEL), layer),
            _layer_resident((D_ATTN, D_MODEL), layer),
            _layer_resident((D_MODEL, D_MODEL), layer),
        ],
        out_specs=pl.BlockSpec((TM, D_MODEL), row),
        out_shape=jax.ShapeDtypeStruct((m, D_MODEL), F32),
        compiler_params=_params(("parallel",)),
        name="mixout",
    )(h, gates, pm, cv, o_prompt, pm_sample, cv_sample, o_sample, wgrp, pscale, pool_proj, conv_proj, attn_proj, w_out)


def _block_diag(w):
    g, c, d = w.shape
    eye = jnp.eye(g, dtype=w.dtype)
    return (eye[:, None, :, None] * w[:, :, None, :]).reshape(g * c, g * d)


def kernel(x_prompt, x_sample, cache_k, cache_v, state_pool, state_conv, meta, ffn1_norm, ffn1_w_gate, ffn1_w_up, ffn1_w_down, mix_norm, w_in, pool_w, pool_scale, pool_proj, conv_w, conv_proj, q_norm, k_norm, attn_proj, w_out, ffn2_norm, ffn2_w_gate, ffn2_w_up, ffn2_w_down):
    n_batch, seq, _ = x_prompt.shape
    n_streams, dec_seq, _ = x_sample.shape
    depth = w_in.shape[0]
    past_len = cache_k.shape[2]
    t_real = N_META + seq
    t_pad = -(-t_real // ATT_BLOCK) * ATT_BLOCK
    rows_p = n_batch * t_pad
    rows_s = n_streams * dec_seq
    m_rows = rows_p + rows_s
    assert m_rows % TM == 0 and rows_s == TM and dec_seq > POOL_STATE and (HALO + dec_seq) % 8 == 0

    tail = jnp.zeros((t_pad - t_real, D_MODEL), F32)
    pieces = [piece for b in range(n_batch) for piece in (meta.astype(F32), x_prompt[b], tail)]
    h = jnp.concatenate(pieces + [x_sample.reshape(rows_s, D_MODEL)], axis=0)

    cache_kt = cache_k.transpose(0, 1, 3, 4, 2).reshape(depth, n_streams, D_ATTN, past_len)
    cache_vt = cache_v.transpose(0, 1, 3, 4, 2).reshape(depth, n_streams, D_ATTN, past_len)

    tri = (np.arange(ATT_BLOCK)[:, None] >= np.arange(ATT_BLOCK)[None, :])
    tri = jnp.asarray(tri, BF16)
    seg = jnp.asarray(np.kron(np.eye(N_HEADS), np.full((HEAD_DIM, HEAD_DIM), 1.0 / HEAD_DIM)), BF16)

    bf = lambda t: t.astype(BF16)
    rowvec = lambda t: t.reshape(depth, 1, -1)
    ffn1 = (rowvec(ffn1_norm), bf(ffn1_w_gate), bf(ffn1_w_up), bf(ffn1_w_down))
    ffn2 = (rowvec(ffn2_norm), bf(ffn2_w_gate), bf(ffn2_w_up), bf(ffn2_w_down))
    w_in_b = bf(w_in)
    mix_gain, q_gain, k_gain = rowvec(mix_norm), rowvec(q_norm), rowvec(k_norm)
    cw = jnp.zeros((depth, 8, D_CONV), F32).at[:, :CONV_W].set(conv_w)
    mix_w = (bf(jnp.stack([_block_diag(pool_w[l]) for l in range(depth)])), rowvec(pool_scale), bf(pool_proj),
             bf(conv_proj), bf(attn_proj), bf(w_out))

    outs = [[] for _ in range(8)]
    for l in range(depth):
        h = _ffn(h, *ffn1, l)
        gates, sq, q, k, kb, v, vb, pm, cv, cx_p = _proj(h, mix_gain, w_in_b, q_gain, k_gain, seg, cw, l, t_pad, t_real)

        o_prompt = _attn_prompt(q, kb, vb, tri, n_batch, t_pad)
        o_sample = _attn_sample(q, kb, vb, cache_kt, cache_vt, l, tri, rows_p, n_streams, dec_seq)

        seq_s = sq[rows_p:].reshape(n_streams, dec_seq, D_SEQ)
        hist = jnp.zeros((n_streams, HALO, D_SEQ), F32)
        hist = hist.at[:, HALO - POOL_STATE:, 0:D_POOL].set(state_pool[l])
        hist = hist.at[:, HALO - (CONV_W - 1):, D_POOL:D_POOL + D_CONV].set(state_conv[l])
        hist = hist.at[:, :, D_POOL + 2 * D_CONV:].set(1.0)
        ext_s = jnp.concatenate([hist, seq_s], axis=1).reshape(n_streams * (HALO + dec_seq), D_SEQ)
        pm_s, cv_s, cx_s = _seqmix_sample(ext_s, cw, l, past_len)
        data = lambda t: t.reshape(n_streams, HALO + dec_seq, -1)[:, HALO:]

        h = _mixout(h, gates, pm, cv, o_prompt, data(pm_s).reshape(rows_s, D_POOL), data(cv_s).reshape(rows_s, D_CONV),
                    o_sample, *mix_w, l)
        if l < depth - 1:
            h = _ffn(h, *ffn2, l)
        else:
            h, y_sample = _ffn(h, *ffn2, l, split_tail=True)

        prompt = lambda t, lo, hi: jnp.stack([t[b * t_pad + lo:b * t_pad + hi] for b in range(n_batch)])
        sample = lambda t: t[rows_p:].reshape(n_streams, dec_seq, -1)
        heads = lambda t: t.reshape(t.shape[0], t.shape[1], N_HEADS, HEAD_DIM)
        outs[0].append(heads(prompt(k, 0, t_real)))
        outs[1].append(heads(prompt(v, 0, t_real)))
        outs[2].append(prompt(sq, t_real - POOL_STATE, t_real)[:, :, :D_POOL])
        outs[3].append(prompt(cx_p, t_real - (CONV_W - 1), t_real))
        outs[4].append(heads(sample(k)))
        outs[5].append(heads(sample(v)))
        outs[6].append(sample(sq)[:, dec_seq - POOL_STATE:, :D_POOL])
        outs[7].append(data(cx_s)[:, dec_seq - (CONV_W - 1):])

    y_prompt = h.reshape(n_batch, t_pad, D_MODEL)[:, N_META:t_real]
    y_sample = y_sample.reshape(n_streams, dec_seq, D_MODEL)
    return (y_prompt, y_sample) + tuple(jnp.stack(o) for o in outs)
```

```python
import functools

import jax
import jax.numpy as jnp
import numpy as np
from jax import lax
from jax.experimental import pallas as pl
from jax.experimental.pallas import tpu as pltpu

F32 = jnp.float32
BF16 = jnp.bfloat16

D_MODEL = 1024
N_META = 16
D_POOL = 256
POOL_WINDOWS = (2, 4, 8, 16)
POOL_GROUP = 64
POOL_STATE = 15
D_CONV = 256
CONV_W = 3
N_HEADS = 8
HEAD_DIM = 64
D_ATTN = N_HEADS * HEAD_DIM
D_FF = 2816
RMS_EPS = 1e-6
LOG2E = 1.4426950408889634
Q_SCALE = HEAD_DIM ** -0.5 * LOG2E
SP_CLAMP = 64.0
CARRY_SETTLED = 160.0
N_GATE = 3 * D_MODEL
D_SEQ = D_POOL + 3 * D_CONV
COL_SEQ, COL_Q, COL_K, COL_V, COL_GATE = 0, D_SEQ, D_SEQ + D_ATTN, D_SEQ + 2 * D_ATTN, D_SEQ + 3 * D_ATTN
IN_COLS = COL_GATE + N_GATE

ATT_BLOCK = 256
ATT_HEADS = 4
HALO = 16
TM = 512
PROJ_CHUNK = 512
VMEM_LIMIT = 56 * 1024 * 1024


def _params(sem):
    return pltpu.CompilerParams(dimension_semantics=sem, vmem_limit_bytes=VMEM_LIMIT)


def _rms_rows(x, gain):
    ms = jnp.mean(x * x, axis=-1, keepdims=True)
    return x * lax.rsqrt(ms + RMS_EPS) * gain


def _ffn_kernel(x_ref, g_ref, wg_ref, wu_ref, wd_ref, o_ref, *tail_ref):
    x = x_ref[...]
    xn = _rms_rows(x, g_ref[0]).astype(BF16)
    g = jnp.dot(xn, wg_ref[0], preferred_element_type=F32)
    u = jnp.dot(xn, wu_ref[0], preferred_element_type=F32)
    a = (g * jax.nn.sigmoid(g) * u).astype(BF16)
    y = x + 0.5 * jnp.dot(a, wd_ref[0], preferred_element_type=F32)
    if not tail_ref:
        o_ref[...] = y
    else:
        last = pl.num_programs(0) - 1

        @pl.when(pl.program_id(0) < last)
        def _():
            o_ref[...] = y

        @pl.when(pl.program_id(0) == last)
        def _():
            tail_ref[0][...] = y


def _resident(shape):
    return pl.BlockSpec(shape, lambda *_: (0,) * len(shape), pipeline_mode=pl.Buffered(1))


def _layer_resident(shape, layer):
    return pl.BlockSpec((1,) + shape, lambda *_: (layer,) + (0,) * len(shape), pipeline_mode=pl.Buffered(1))


def _ffn(x, gain, wg, wu, wd, layer, split_tail=False):
    m = x.shape[0]
    tiles = m // TM
    row = pl.BlockSpec((TM, D_MODEL), lambda i: (i, 0))
    if split_tail:
        out_specs = [pl.BlockSpec((TM, D_MODEL), lambda i: (jnp.minimum(i, tiles - 2), 0)),
                     pl.BlockSpec((TM, D_MODEL), lambda i: (0, 0))]
        out_shape = [jax.ShapeDtypeStruct((m - TM, D_MODEL), F32), jax.ShapeDtypeStruct((TM, D_MODEL), F32)]
    else:
        out_specs, out_shape = row, jax.ShapeDtypeStruct((m, D_MODEL), F32)
    return pl.pallas_call(
        _ffn_kernel,
        grid=(tiles,),
        in_specs=[
            row,
            _layer_resident((1, D_MODEL), layer),
            _layer_resident((D_MODEL, D_FF), layer),
            _layer_resident((D_MODEL, D_FF), layer),
            _layer_resident((D_FF, D_MODEL), layer),
        ],
        out_specs=out_specs,
        out_shape=out_shape,
        compiler_params=_params(("arbitrary",) if split_tail else ("parallel",)),
        name="ffn",
    )(x, gain, wg, wu, wd)


def _pool_conv(ext, pos, cw):
    a = ext[:, 0:D_POOL]
    cx = ext[:, D_POOL + 2 * D_CONV:D_POOL + 3 * D_CONV] * ext[:, D_POOL:D_POOL + D_CONV]
    sums = []
    s = a
    for w in POOL_WINDOWS:
        s = s + pltpu.roll(s, w // 2, 0)
        sums.append(s)
    lane_group = lax.broadcasted_iota(jnp.int32, a.shape, 1) // POOL_GROUP
    mean = jnp.zeros_like(a)
    for g, w in enumerate(POOL_WINDOWS):
        cnt = jnp.clip(pos + 1, 1, w).astype(F32)
        mean = jnp.where(lane_group == g, sums[g] / cnt, mean)
    y = pltpu.roll(cx, 2, 0) * cw[0:1] + pltpu.roll(cx, 1, 0) * cw[1:2] + cx * cw[2:3]
    gated = ext[:, D_POOL + D_CONV:D_POOL + 2 * D_CONV] * y
    return (mean - a)[HALO:], gated[HALO:], cx[HALO:]


def _head_rms(p, seg, gain):
    ms = jnp.dot((p * p).astype(BF16), seg, preferred_element_type=F32)
    return p * lax.rsqrt(ms + RMS_EPS) * gain


def _proj_kernel(h_ref, g_ref, w_ref, qn_ref, kn_ref, seg_ref, cw_ref,
                 gate_ref, seq_ref, q_ref, k_ref, kb_ref, v_ref, vb_ref, pm_ref, cv_ref, cx_ref, halo_ref,
                 *, t_pad, t_real):
    i = pl.program_id(0)

    @pl.when(i == 0)
    def _():
        halo_ref[...] = jnp.zeros_like(halo_ref)

    u = _rms_rows(h_ref[...], g_ref[0]).astype(BF16)

    def project(col, width):
        return jnp.dot(u, w_ref[0, :, col:col + width], preferred_element_type=F32)

    ch = PROJ_CHUNK
    for c in range(N_GATE // ch):
        gate_ref[:, c * ch:(c + 1) * ch] = jax.nn.sigmoid(project(COL_GATE + c * ch, ch)).astype(BF16)
    for c in range(D_SEQ // ch):
        seq_ref[:, c * ch:(c + 1) * ch] = project(COL_SEQ + c * ch, ch)
    q = _head_rms(project(COL_Q, D_ATTN), seg_ref[...], qn_ref[0])
    q_ref[...] = (q * Q_SCALE).astype(BF16)
    k = _head_rms(project(COL_K, D_ATTN), seg_ref[...], kn_ref[0])
    k_ref[...] = k
    kb_ref[...] = k.astype(BF16)
    v = project(COL_V, D_ATTN)
    v_ref[...] = v
    vb_ref[...] = v.astype(BF16)

    tm = seq_ref.shape[0]
    row0 = i * tm
    pos0 = row0 - (row0 // t_pad) * t_pad
    pos = pos0 + lax.broadcasted_iota(jnp.int32, (tm, 1), 0)
    pos = jnp.where(pos >= t_pad, pos - t_pad, pos)
    seq = jnp.where(pos < t_real, seq_ref[...], 0.0)
    ext = jnp.concatenate([halo_ref[...], seq], axis=0)
    pos_ext = jnp.concatenate([jnp.zeros((HALO, 1), jnp.int32), pos], axis=0)
    pm, cv, cx = _pool_conv(ext, pos_ext, cw_ref[0])
    pm_ref[...] = pm.astype(BF16)
    cv_ref[...] = cv.astype(BF16)
    cx_ref[...] = cx
    halo_ref[...] = seq[tm - HALO:]


def _proj(h, gain, w, qn, kn, seg, conv_w, layer, t_pad, t_real):
    m = h.shape[0]
    assert TM <= t_pad and t_pad - t_real >= HALO
    row = lambda i: (i, 0)
    widths = [(N_GATE, BF16), (D_SEQ, F32), (D_ATTN, BF16), (D_ATTN, F32), (D_ATTN, BF16), (D_ATTN, F32), (D_ATTN, BF16),
              (D_POOL, BF16), (D_CONV, BF16), (D_CONV, F32)]
    return pl.pallas_call(
        functools.partial(_proj_kernel, t_pad=t_pad, t_real=t_real),
        grid=(m // TM,),
        in_specs=[
            pl.BlockSpec((TM, D_MODEL), row),
            _layer_resident((1, D_MODEL), layer),
            _layer_resident((D_MODEL, IN_COLS), layer),
            _layer_resident((1, D_ATTN), layer),
            _layer_resident((1, D_ATTN), layer),
            _resident((D_ATTN, D_ATTN)),
            _layer_resident((8, D_CONV), layer),
        ],
        out_specs=[pl.BlockSpec((TM, w), row) for w, _ in widths],
        out_shape=[jax.ShapeDtypeStruct((m, w), dt) for w, dt in widths],
        scratch_shapes=[pltpu.VMEM((HALO, D_SEQ), F32)],
        compiler_params=_params(("arbitrary",)),
        name="proj",
    )(h, gain, w, qn, kn, seg, conv_w)


def _scores(qs, kblk):
    rows = qs[0].shape[0]
    z = lax.dot_general(jnp.concatenate(qs, axis=0), kblk, (((1,), (1,)), ((), ())), preferred_element_type=F32)
    return [z[h * rows:(h + 1) * rows] for h in range(len(qs))]


def _softplus_bits(zs, valid):
    out = []
    for z in zs:
        sp = jnp.maximum(jnp.log(1.0 + jnp.exp2(jnp.minimum(z, SP_CLAMP))) * LOG2E, z)
        if valid is not None:
            sp = jnp.where(valid, sp, 0.0)
        out.append(sp.astype(BF16))
    return out


def _stick_weights(zs, sps, tri, carries, valid):
    cums = [jnp.dot(sp, tri, preferred_element_type=F32) for sp in sps]
    probs = []
    for z, cum, carry in zip(zs, cums, carries):
        a = jnp.exp2(z - cum - carry)
        if valid is not None:
            a = jnp.where(valid, a, 0.0)
        probs.append(a.astype(BF16))
    return probs, [carry + cum[:, 0:1] for carry, cum in zip(carries, cums)]


def _attn_prompt_kernel(q_ref, k_ref, v_ref, tri_ref, o_ref, acc_ref, car_ref, p_ref):
    i = pl.program_id(2)
    blk = ATT_BLOCK
    nh = ATT_HEADS
    lane_head = lax.broadcasted_iota(jnp.int32, (blk, nh * HEAD_DIM), 1) // HEAD_DIM
    q2 = q_ref[...]
    qs = [jnp.where(lane_head == h, q2, jnp.zeros_like(q2)) for h in range(nh)]
    tri = tri_ref[...]

    def scores(j):
        kblk = k_ref[pl.ds(pl.multiple_of(j * blk, blk), blk), :]
        return _scores(qs, kblk)

    def weighted_values(j):
        vblk = v_ref[pl.ds(pl.multiple_of(j * blk, blk), blk), :]
        vstack = jnp.concatenate([jnp.where(lane_head == h, vblk, jnp.zeros_like(vblk)) for h in range(nh)], axis=0)
        return jnp.dot(p_ref[...], vstack, preferred_element_type=F32)

    def weights(zs, sps, valid):
        probs, cars = _stick_weights(zs, sps, tri, [car_ref[h] for h in range(nh)], valid)
        for h in range(nh):
            p_ref[:, h * blk:(h + 1) * blk] = probs[h]
            car_ref[h] = cars[h]
        return jnp.min(functools.reduce(jnp.minimum, cars)) >= CARRY_SETTLED

    car_ref[...] = jnp.zeros_like(car_ref)
    row = lax.broadcasted_iota(jnp.int32, (blk, blk), 0)
    col = lax.broadcasted_iota(jnp.int32, (blk, blk), 1)
    zs = scores(i)
    settled = weights(zs, _softplus_bits(zs, col < row), col < row)
    acc_ref[...] = jnp.zeros_like(acc_ref)

    def body(state):
        t, _ = state
        j = i - t
        zs = scores(j)
        acc_ref[...] += weighted_values(j + 1)
        settled = weights(zs, _softplus_bits(zs, None), None)
        return t + 1, settled

    t_end, _ = lax.while_loop(lambda state: (state[0] <= i) & jnp.logical_not(state[1]), body,
                              (jnp.int32(1), settled))
    o_ref[...] = (acc_ref[...] + weighted_values(i - t_end + 1)).astype(o_ref.dtype)


def _attn_prompt(q, kb, vb, tri, n_batch, t_pad):
    nq = t_pad // ATT_BLOCK
    pair = ATT_HEADS * HEAD_DIM
    return pl.pallas_call(
        _attn_prompt_kernel,
        grid=(n_batch, N_HEADS // ATT_HEADS, nq),
        in_specs=[
            pl.BlockSpec((ATT_BLOCK, pair), lambda b, h, i: (b * nq + i, h)),
            pl.BlockSpec((t_pad, pair), lambda b, h, i: (b, h)),
            pl.BlockSpec((t_pad, pair), lambda b, h, i: (b, h)),
            pl.BlockSpec((ATT_BLOCK, ATT_BLOCK), lambda b, h, i: (0, 0)),
        ],
        out_specs=pl.BlockSpec((ATT_BLOCK, pair), lambda b, h, i: (b * nq + i, h)),
        out_shape=jax.ShapeDtypeStruct((n_batch * t_pad, D_ATTN), BF16),
        scratch_shapes=[pltpu.VMEM((ATT_BLOCK, pair), F32), pltpu.VMEM((ATT_HEADS, ATT_BLOCK, 1), F32),
                        pltpu.VMEM((ATT_BLOCK, ATT_HEADS * ATT_BLOCK), BF16)],
        compiler_params=_params(("parallel", "parallel", "arbitrary")),
        name="attn_prompt",
    )(q, kb, vb, tri)


def _attn_sample_kernel(*refs, dec_seq, n_blocks, first):
    blk = ATT_BLOCK
    rows = N_HEADS * dec_seq
    if first:
        q_ref, kn_ref, vn_ref, kc_ref, vc_ref, tri_ref, o_ref, acc_ref, car_ref = refs
    else:
        q_ref, kc_ref, vc_ref, tri_ref, acc_in_ref, car_in_ref, o_ref, acc_ref, car_ref = refs
    q = q_ref[...]
    lane_head = lax.broadcasted_iota(jnp.int32, (dec_seq, D_ATTN), 1) // HEAD_DIM
    qs = jnp.concatenate([jnp.where(lane_head == h, q, jnp.zeros_like(q)) for h in range(N_HEADS)], axis=0)
    tri = tri_ref[...]

    if first:
        pad = jnp.zeros((blk - dec_seq, D_ATTN), BF16)
        k_new = jnp.concatenate([kn_ref[...], pad], axis=0)
        v_new = jnp.concatenate([vn_ref[...], pad], axis=0)
        tq = lax.broadcasted_iota(jnp.int32, (rows, blk), 0) % dec_seq
        col = lax.broadcasted_iota(jnp.int32, (rows, blk), 1)
        zs = _scores([qs], k_new)
        (a,), (car,) = _stick_weights(zs, _softplus_bits(zs, col < tq), tri, [jnp.zeros((rows, 1), F32)], col < tq)
        acc = jnp.dot(a, v_new, preferred_element_type=F32)
    else:
        acc, car = acc_in_ref[...], car_in_ref[...]
    for j in reversed(range(n_blocks)):
        kt = kc_ref[0, 0, :, j * blk:(j + 1) * blk].astype(BF16)
        vt = vc_ref[0, 0, :, j * blk:(j + 1) * blk].astype(BF16)
        zs = [jnp.dot(qs, kt, preferred_element_type=F32)]
        (a,), (car,) = _stick_weights(zs, _softplus_bits(zs, None), tri, [car], None)
        acc = acc + lax.dot_general(a, vt, (((1,), (1,)), ((), ())), preferred_element_type=F32)
    acc_ref[...] = acc
    car_ref[...] = car
    o = jnp.zeros((dec_seq, D_ATTN), F32)
    for h in range(N_HEADS):
        o = o + jnp.where(lane_head == h, acc[h * dec_seq:(h + 1) * dec_seq, :], 0.0)
    o_ref[...] = o.astype(o_ref.dtype)


def _attn_sample_call(q, new_kv, cache_kt, cache_vt, layer, key0, n, tri, state, row0, n_streams, dec_seq):
    assert n % ATT_BLOCK == 0 and key0 % n == 0 and row0 % dec_seq == 0 and dec_seq % 16 == 0
    blk0 = row0 // dec_seq
    rows = N_HEADS * dec_seq
    new_rows = pl.BlockSpec((dec_seq, D_ATTN), lambda b: (blk0 + b, 0))
    cache = pl.BlockSpec((1, 1, D_ATTN, n), lambda b: (layer, b, 0, key0 // n))
    tri_spec = pl.BlockSpec((ATT_BLOCK, ATT_BLOCK), lambda b: (0, 0))
    acc_spec = pl.BlockSpec((rows, D_ATTN), lambda b: (b, 0))
    car_spec = pl.BlockSpec((rows, 1), lambda b: (b, 0))
    if state is None:
        operands = (q,) + tuple(new_kv) + (cache_kt, cache_vt, tri)
        in_specs = [new_rows, new_rows, new_rows, cache, cache, tri_spec]
    else:
        operands = (q, cache_kt, cache_vt, tri) + tuple(state)
        in_specs = [new_rows, cache, cache, tri_spec, acc_spec, car_spec]
    return pl.pallas_call(
        functools.partial(_attn_sample_kernel, dec_seq=dec_seq, n_blocks=n // ATT_BLOCK, first=state is None),
        grid=(n_streams,),
        in_specs=in_specs,
        out_specs=[pl.BlockSpec((dec_seq, D_ATTN), lambda b: (b, 0)), acc_spec, car_spec],
        out_shape=[jax.ShapeDtypeStruct((n_streams * dec_seq, D_ATTN), BF16),
                   jax.ShapeDtypeStruct((n_streams * rows, D_ATTN), F32),
                   jax.ShapeDtypeStruct((n_streams * rows, 1), F32)],
        compiler_params=_params(("parallel",)),
        name="attn_sample",
    )(*operands)


def _attn_sample(q, kb, vb, cache_kt, cache_vt, layer, tri, row0, n_streams, dec_seq):
    past_len = cache_kt.shape[3]
    split = past_len - ATT_BLOCK
    o, acc, car = _attn_sample_call(q, (kb, vb), cache_kt, cache_vt, layer, split, ATT_BLOCK, tri, None,
                                    row0, n_streams, dec_seq)
    if split == 0:
        return o

    def earlier_keys(_):
        return _attn_sample_call(q, None, cache_kt, cache_vt, layer, 0, split, tri, (acc, car),
                                 row0, n_streams, dec_seq)[0]

    return lax.cond(jnp.min(car) >= CARRY_SETTLED, lambda _: o, earlier_keys, None)


def _seqmix_kernel(ext_ref, cw_ref, pm_ref, cv_ref, cx_ref, *, pos0):
    ext = ext_ref[...]
    ext = jnp.concatenate([jnp.zeros((HALO, D_SEQ), F32), ext], axis=0)
    pos = jnp.full((ext.shape[0], 1), pos0, jnp.int32)
    pm, cv, cx = _pool_conv(ext, pos, cw_ref[0])
    pm_ref[...] = pm.astype(BF16)
    cv_ref[...] = cv.astype(BF16)
    cx_ref[...] = cx


def _seqmix_sample(ext, conv_w, layer, pos0):
    rows = ext.shape[0]
    full = lambda w: pl.BlockSpec((rows, w), lambda i: (0, 0))
    out = lambda dt: jax.ShapeDtypeStruct((rows, D_POOL), dt)
    return pl.pallas_call(
        functools.partial(_seqmix_kernel, pos0=pos0),
        grid=(1,),
        in_specs=[full(D_SEQ), pl.BlockSpec((1, 8, D_CONV), lambda i: (layer, 0, 0))],
        out_specs=[full(D_POOL)] * 3,
        out_shape=[out(BF16), out(BF16), out(F32)],
        compiler_params=_params(("arbitrary",)),
        name="seqmix",
    )(ext, conv_w)


def _mixout_kernel(h_ref, g_ref, pm_ref, cv_ref, o_ref, pms_ref, cvs_ref, os_ref, wgrp_ref, ps_ref, pp_ref, cp_ref,
                   ap_ref, wo_ref, out_ref):
    tail = pl.program_id(0) == pl.num_programs(0) - 1
    pm = jnp.where(tail, pms_ref[...], pm_ref[...])
    cv = jnp.where(tail, cvs_ref[...], cv_ref[...])
    o = jnp.where(tail, os_ref[...], o_ref[...])
    ya = jnp.dot(pm, wgrp_ref[0], preferred_element_type=F32) * ps_ref[0]
    ya = jnp.dot(ya.astype(BF16), pp_ref[0], preferred_element_type=F32)
    yb = jnp.dot(cv, cp_ref[0], preferred_element_type=F32)
    yc = jnp.dot(o, ap_ref[0], preferred_element_type=F32)
    g = g_ref[...].astype(F32)
    mixed = g[:, 0:D_MODEL] * ya + g[:, D_MODEL:2 * D_MODEL] * yb + g[:, 2 * D_MODEL:3 * D_MODEL] * yc
    out_ref[...] = h_ref[...] + jnp.dot(mixed.astype(BF16), wo_ref[0], preferred_element_type=F32)


def _mixout(h, gates, pm, cv, o_prompt, pm_sample, cv_sample, o_sample, wgrp, pscale, pool_proj, conv_proj, attn_proj,
            w_out, layer):
    m = h.shape[0]
    n_prompt_tiles = m // TM - 1
    assert o_prompt.shape[0] == m - TM and all(t.shape[0] == TM for t in (pm_sample, cv_sample, o_sample))
    row = lambda i: (i, 0)
    prompt_row = lambda i: (jnp.minimum(i, n_prompt_tiles - 1), 0)
    const = lambda i: (0, 0)
    return pl.pallas_call(
        _mixout_kernel,
        grid=(m // TM,),
        in_specs=[
            pl.BlockSpec((TM, D_MODEL), row),
            pl.BlockSpec((TM, N_GATE), row),
            pl.BlockSpec((TM, D_POOL), prompt_row),
            pl.BlockSpec((TM, D_CONV), prompt_row),
            pl.BlockSpec((TM, D_ATTN), prompt_row),
            pl.BlockSpec((TM, D_POOL), const),
            pl.BlockSpec((TM, D_CONV), const),
            pl.BlockSpec((TM, D_ATTN), const),
            _layer_resident((D_POOL, D_POOL), layer),
            _layer_resident((1, D_POOL), layer),
            _layer_resident((D_POOL, D_MODEL), layer),
            _layer_resident((D_CONV, D_MODEL), layer),
            _layer_resident((D_ATTN, D_MODEL), layer),
            _layer_resident((D_MODEL, D_MODEL), layer),
        ],
        out_specs=pl.BlockSpec((TM, D_MODEL), row),
        out_shape=jax.ShapeDtypeStruct((m, D_MODEL), F32),
        compiler_params=_params(("parallel",)),
        name="mixout",
    )(h, gates, pm, cv, o_prompt, pm_sample, cv_sample, o_sample, wgrp, pscale, pool_proj, conv_proj, attn_proj, w_out)


def _block_diag(w):
    g, c, d = w.shape
    eye = jnp.eye(g, dtype=w.dtype)
    return (eye[:, None, :, None] * w[:, :, None, :]).reshape(g * c, g * d)


def kernel(x_prompt, x_sample, cache_k, cache_v, state_pool, state_conv, meta, ffn1_norm, ffn1_w_gate, ffn1_w_up, ffn1_w_down, mix_norm, w_in, pool_w, pool_scale, pool_proj, conv_w, conv_proj, q_norm, k_norm, attn_proj, w_out, ffn2_norm, ffn2_w_gate, ffn2_w_up, ffn2_w_down):
    n_batch, seq, _ = x_prompt.shape
    n_streams, dec_seq, _ = x_sample.shape
    depth = w_in.shape[0]
    past_len = cache_k.shape[2]
    t_real = N_META + seq
    t_pad = -(-t_real // ATT_BLOCK) * ATT_BLOCK
    rows_p = n_batch * t_pad
    rows_s = n_streams * dec_seq
    m_rows = rows_p + rows_s
    assert m_rows % TM == 0 and rows_s == TM and dec_seq > POOL_STATE and (HALO + dec_seq) % 8 == 0

    tail = jnp.zeros((t_pad - t_real, D_MODEL), F32)
    pieces = [piece for b in range(n_batch) for piece in (meta.astype(F32), x_prompt[b], tail)]
    h = jnp.concatenate(pieces + [x_sample.reshape(rows_s, D_MODEL)], axis=0)

    cache_kt = cache_k.transpose(0, 1, 3, 4, 2).reshape(depth, n_streams, D_ATTN, past_len)
    cache_vt = cache_v.transpose(0, 1, 3, 4, 2).reshape(depth, n_streams, D_ATTN, past_len)

    tri = (np.arange(ATT_BLOCK)[:, None] >= np.arange(ATT_BLOCK)[None, :])
    tri = jnp.asarray(tri, BF16)
    seg = jnp.asarray(np.kron(np.eye(N_HEADS), np.full((HEAD_DIM, HEAD_DIM), 1.0 / HEAD_DIM)), BF16)

    bf = lambda t: t.astype(BF16)
    rowvec = lambda t: t.reshape(depth, 1, -1)
    ffn1 = (rowvec(ffn1_norm), bf(ffn1_w_gate), bf(ffn1_w_up), bf(ffn1_w_down))
    ffn2 = (rowvec(ffn2_norm), bf(ffn2_w_gate), bf(ffn2_w_up), bf(ffn2_w_down))
    w_in_b = bf(w_in)
    mix_gain, q_gain, k_gain = rowvec(mix_norm), rowvec(q_norm), rowvec(k_norm)
    cw = jnp.zeros((depth, 8, D_CONV), F32).at[:, :CONV_W].set(conv_w)
    mix_w = (bf(jnp.stack([_block_diag(pool_w[l]) for l in range(depth)])), rowvec(pool_scale), bf(pool_proj),
             bf(conv_proj), bf(attn_proj), bf(w_out))

    outs = [[] for _ in range(8)]
    for l in range(depth):
        h = _ffn(h, *ffn1, l)
        gates, sq, q, k, kb, v, vb, pm, cv, cx_p = _proj(h, mix_gain, w_in_b, q_gain, k_gain, seg, cw, l, t_pad, t_real)

        o_prompt = _attn_prompt(q, kb, vb, tri, n_batch, t_pad)
        o_sample = _attn_sample(q, kb, vb, cache_kt, cache_vt, l, tri, rows_p, n_streams, dec_seq)

        seq_s = sq[rows_p:].reshape(n_streams, dec_seq, D_SEQ)
        hist = jnp.zeros((n_streams, HALO, D_SEQ), F32)
        hist = hist.at[:, HALO - POOL_STATE:, 0:D_POOL].set(state_pool[l])
        hist = hist.at[:, HALO - (CONV_W - 1):, D_POOL:D_POOL + D_CONV].set(state_conv[l])
        hist = hist.at[:, :, D_POOL + 2 * D_CONV:].set(1.0)
        ext_s = jnp.concatenate([hist, seq_s], axis=1).reshape(n_streams * (HALO + dec_seq), D_SEQ)
        pm_s, cv_s, cx_s = _seqmix_sample(ext_s, cw, l, past_len)
        data = lambda t: t.reshape(n_streams, HALO + dec_seq, -1)[:, HALO:]

        h = _mixout(h, gates, pm, cv, o_prompt, data(pm_s).reshape(rows_s, D_POOL), data(cv_s).reshape(rows_s, D_CONV),
                    o_sample, *mix_w, l)
        if l < depth - 1:
            h = _ffn(h, *ffn2, l)
        else:
            h, y_sample = _ffn(h, *ffn2, l, split_tail=True)

        prompt = lambda t, lo, hi: jnp.stack([t[b * t_pad + lo:b * t_pad + hi] for b in range(n_batch)])
        sample = lambda t: t[rows_p:].reshape(n_streams, dec_seq, -1)
        heads = lambda t: t.reshape(t.shape[0], t.shape[1], N_HEADS, HEAD_DIM)
        outs[0].append(heads(prompt(k, 0, t_real)))
        outs[1].append(heads(prompt(v, 0, t_real)))
        outs[2].append(prompt(sq, t_real - POOL_STATE, t_real)[:, :, :D_POOL])
        outs[3].append(prompt(cx_p, t_real - (CONV_W - 1), t_real))
        outs[4].append(heads(sample(k)))
        outs[5].append(heads(sample(v)))
        outs[6].append(sample(sq)[:, dec_seq - POOL_STATE:, :D_POOL])
        outs[7].append(data(cx_s)[:, dec_seq - (CONV_W - 1):])

    y_prompt = h.reshape(n_batch, t_pad, D_MODEL)[:, N_META:t_real]
    y_sample = y_sample.reshape(n_streams, dec_seq, D_MODEL)
    return (y_prompt, y_sample) + tuple(jnp.stack(o) for o in outs)
```
